```python
import jax, jax.numpy as jnp
from jax import lax
import numpy as np

D_MODEL = 4096
BATCH = 1
SEQ = 8192
DEPTH = 2

EPS = 1e-6
SC_WIDTH = 2048
SC_KERNEL = 3
CF_WIDTH = 2048
CF_KERNEL = 31
HEAD_DIM = 128
DW_PATTERNS = ((128, 1), (512, 4), (2048, 16))
N_DW_GROUPS = len(DW_PATTERNS)
DW_HEADS = 8
DW_WIDTH = N_DW_GROUPS * DW_HEADS * HEAD_DIM
DW_OUT = DW_HEADS * HEAD_DIM
ROT_DIM = HEAD_DIM // 4
ROPE_THETA = 500000.0
N_BRANCH = 3
IN_SIZES = (SC_WIDTH, SC_WIDTH, SC_WIDTH, CF_WIDTH, CF_WIDTH, DW_WIDTH, DW_WIDTH, DW_WIDTH,
            D_MODEL, D_MODEL, D_MODEL)
IN_COLS = sum(IN_SIZES)
IN_SPLITS = [int(i) for i in np.cumsum(IN_SIZES)[:-1]]
D_FF = 4 * D_MODEL
N_MOD = 6

kernel_name = "hybrid_gated_conv_conformer_dilated_attn_block"


def rmsnorm(x, g):
    xf = x.astype(jnp.float32)
    y = xf * lax.rsqrt(jnp.mean(xf * xf, axis=-1, keepdims=True) + EPS)
    return (y * g.astype(jnp.float32)).astype(x.dtype)


def layernorm(x, g, b):
    xf = x.astype(jnp.float32)
    mu = jnp.mean(xf, axis=-1, keepdims=True)
    var = jnp.mean(jnp.square(xf - mu), axis=-1, keepdims=True)
    y = (xf - mu) * lax.rsqrt(var + EPS)
    return (y * g.astype(jnp.float32) + b.astype(jnp.float32)).astype(x.dtype)


def causal_depthwise_conv(x, w):
    k = w.shape[0]
    return lax.conv_general_dilated(
        x, w[:, None, :].astype(x.dtype), window_strides=(1,), padding=[(k - 1, 0)],
        dimension_numbers=("NWC", "WIO", "NWC"), feature_group_count=x.shape[-1])


def partial_rope(x, positions):
    half = ROT_DIM // 2
    inv_freq = ROPE_THETA ** (-jnp.arange(0, ROT_DIM, 2, dtype=jnp.float32) / ROT_DIM)
    ang = positions.astype(jnp.float32)[..., None] * inv_freq
    cos = jnp.cos(ang)[:, :, None, None, :]
    sin = jnp.sin(ang)[:, :, None, None, :]
    xf = x.astype(jnp.float32)
    x1, x2, rest = xf[..., :half], xf[..., half:ROT_DIM], xf[..., ROT_DIM:]
    out = jnp.concatenate([x1 * cos - x2 * sin, x2 * cos + x1 * sin, rest], axis=-1)
    return out.astype(x.dtype)


def dilated_window_attention(q, k, v, window, dilation):
    b, t, h, dh = q.shape
    n = window // dilation
    span = n * dilation
    tp = -(-t // span) * span
    nb = tp // span
    sub = tp // dilation

    def to_blocks(a):
        a = jnp.pad(a, ((0, 0), (0, tp - t), (0, 0), (0, 0)))
        a = a.reshape(b, sub, dilation, h, dh).transpose(0, 2, 1, 3, 4)
        return a.reshape(b, dilation, nb, n, h, dh)

    def band(a):
        prev = jnp.pad(a, ((0, 0), (0, 0), (1, 0), (0, 0), (0, 0), (0, 0)))[:, :, :-1]
        return jnp.concatenate([prev, a], axis=3)

    qb, kb, vb = to_blocks(q), band(to_blocks(k)), band(to_blocks(v))
    scores = jnp.einsum("brnqhe,brnkhe->brnhqk", qb, kb).astype(jnp.float32) * (dh ** -0.5)
    qi = jnp.arange(n)[:, None]
    kj = jnp.arange(2 * n)[None, :]
    in_window = (kj >= qi) & (kj <= qi + n)
    first = (jnp.arange(nb) == 0)[:, None, None]
    valid = in_window[None] & ~(first & (kj < n)[None])
    scores = jnp.where(valid[None, None, :, None], scores, -jnp.inf)
    m = jnp.max(scores, axis=-1, keepdims=True)
    p = jnp.exp(scores - m)
    s = jnp.sum(p, axis=-1, keepdims=True)
    o = jnp.einsum("brnhqk,brnkhe->brnqhe", (p / s).astype(v.dtype), vb)
    lse = (m + jnp.log(s))[..., 0]
    o = o.reshape(b, dilation, sub, h, dh).transpose(0, 2, 1, 3, 4).reshape(b, tp, h, dh)[:, :t]
    lse = lse.transpose(0, 1, 2, 4, 3).reshape(b, dilation, sub, h)
    lse = lse.transpose(0, 2, 1, 3).reshape(b, tp, h)[:, :t]
    return o, lse


def setup_inputs(seed: int = 0) -> dict:
    key = jax.random.key(seed)
    ks = jax.random.split(key, 24)
    f32 = jnp.float32

    def nrm(k, shape, scale):
        return jax.random.normal(k, shape, f32) * scale

    x = nrm(ks[0], (BATCH, SEQ, D_MODEL), 1.0)
    c = nrm(ks[1], (BATCH, D_MODEL), 1.0)
    offset = jax.random.randint(ks[2], (BATCH, 1), 0, 4096, dtype=jnp.int32)
    positions = offset + jnp.arange(SEQ, dtype=jnp.int32)[None, :]
    return {
        "x": x,
        "c": c,
        "positions": positions,
        "w_ada": nrm(ks[3], (DEPTH, D_MODEL, N_MOD * D_MODEL), D_MODEL ** -0.5),
        "b_ada": nrm(ks[4], (DEPTH, N_MOD * D_MODEL), 0.02),
        "g_mix": 1.0 + nrm(ks[5], (DEPTH, D_MODEL), 0.02),
        "w_in": nrm(ks[6], (DEPTH, D_MODEL, IN_COLS), D_MODEL ** -0.5),
        "conv_a": nrm(ks[7], (DEPTH, SC_KERNEL, SC_WIDTH), SC_KERNEL ** -0.5),
        "conv_b": nrm(ks[8], (DEPTH, CF_KERNEL, CF_WIDTH), CF_KERNEL ** -0.5),
        "conv_b_bias": nrm(ks[9], (DEPTH, CF_WIDTH), 0.02),
        "ln_cf_g": 1.0 + nrm(ks[10], (DEPTH, CF_WIDTH), 0.02),
        "ln_cf_b": nrm(ks[11], (DEPTH, CF_WIDTH), 0.02),
        "w_out_a": nrm(ks[12], (DEPTH, SC_WIDTH, D_MODEL), SC_WIDTH ** -0.5),
        "w_out_b": nrm(ks[13], (DEPTH, CF_WIDTH, D_MODEL), CF_WIDTH ** -0.5),
        "w_out_c": nrm(ks[14], (DEPTH, DW_OUT, D_MODEL), DW_OUT ** -0.5),
        "w_o": nrm(ks[15], (DEPTH, D_MODEL, D_MODEL), D_MODEL ** -0.5),
        "g_mlp": 1.0 + nrm(ks[16], (DEPTH, D_MODEL), 0.02),
        "w_mlp1": nrm(ks[17], (DEPTH, D_MODEL, D_FF), D_MODEL ** -0.5),
        "w_mlp2": nrm(ks[18], (DEPTH, D_FF, D_MODEL), D_FF ** -0.5),
        "g_final": 1.0 + nrm(ks[19], (D_MODEL,), 0.02),
    }


def reference(x, c, positions, w_ada, b_ada, g_mix, w_in, conv_a, conv_b, conv_b_bias, ln_cf_g,
              ln_cf_b, w_out_a, w_out_b, w_out_c, w_o, g_mlp, w_mlp1, w_mlp2, g_final):
    b, t, _ = x.shape
    c_act = jax.nn.silu(c)
    for l in range(DEPTH):
        mod = (c_act @ w_ada[l] + b_ada[l])[:, None, :]
        shift1, scale1, gate1, shift2, scale2, gate2 = jnp.split(mod, N_MOD, axis=-1)

        h = rmsnorm(x, g_mix[l]) * (1.0 + scale1) + shift1
        proj = h @ w_in[l]
        (sc_b, sc_c, sc_x, cf_a, cf_g, dw_q, dw_k, dw_v,
         gate_a, gate_b, gate_c) = jnp.split(proj, IN_SPLITS, axis=-1)

        y_a = (sc_b * causal_depthwise_conv(sc_c * sc_x, conv_a[l])) @ w_out_a[l]

        u = cf_a * jax.nn.sigmoid(cf_g)
        u = causal_depthwise_conv(u, conv_b[l]) + conv_b_bias[l]
        u = jax.nn.silu(layernorm(u, ln_cf_g[l], ln_cf_b[l]))
        y_b = u @ w_out_b[l]

        q = partial_rope(dw_q.reshape(b, t, N_DW_GROUPS, DW_HEADS, HEAD_DIM), positions)
        k = partial_rope(dw_k.reshape(b, t, N_DW_GROUPS, DW_HEADS, HEAD_DIM), positions)
        v = dw_v.reshape(b, t, N_DW_GROUPS, DW_HEADS, HEAD_DIM)
        outs, lses = [], []
        for g, (window, dilation) in enumerate(DW_PATTERNS):
            o_g, lse_g = dilated_window_attention(q[:, :, g], k[:, :, g], v[:, :, g], window, dilation)
            outs.append(o_g)
            lses.append(lse_g)
        wts = jax.nn.softmax(jnp.stack(lses, axis=0), axis=0)
        o = jnp.sum(wts[..., None].astype(v.dtype) * jnp.stack(outs, axis=0), axis=0)
        y_c = o.reshape(b, t, DW_OUT) @ w_out_c[l]

        merged = (jax.nn.sigmoid(gate_a) * y_a + jax.nn.sigmoid(gate_b) * y_b
                  + jax.nn.sigmoid(gate_c) * y_c)
        x = x + gate1 * (merged @ w_o[l])

        h2 = rmsnorm(x, g_mlp[l]) * (1.0 + scale2) + shift2
        x = x + gate2 * (jnp.square(jax.nn.relu(h2 @ w_mlp1[l])) @ w_mlp2[l])

    return rmsnorm(x, g_final)
```

```python
import functools

import jax
import jax.numpy as jnp
from jax import lax
from jax.experimental import pallas as pl
from jax.experimental.pallas import tpu as pltpu

F32 = jnp.float32
BF16 = jnp.bfloat16

EPS = 1e-6
SC_WIDTH = 2048
SC_KERNEL = 3
CF_WIDTH = 2048
CF_KERNEL = 31
HEAD_DIM = 128
DW_PATTERNS = ((128, 1), (512, 4), (2048, 16))
N_DW_GROUPS = len(DW_PATTERNS)
DW_HEADS = 8
DW_GROUP_WIDTH = DW_HEADS * HEAD_DIM
DW_WIDTH = N_DW_GROUPS * DW_GROUP_WIDTH
ATTN_BLOCK = 128
ROT_DIM = HEAD_DIM // 4
ROT_HALF = ROT_DIM // 2
ROPE_THETA = 500000.0
N_MOD = 6

LANES = 128
SUBLANES = 8
CF_HALO = 32
VMEM_LIMIT = 56 * 1024 * 1024


def _params(n_axes):
    return pltpu.CompilerParams(dimension_semantics=("arbitrary",) * n_axes,
                                vmem_limit_bytes=VMEM_LIMIT)


def _dot(a, b):
    return jnp.dot(a, b, preferred_element_type=F32)


def _sigmoid(v):
    return jax.nn.sigmoid(v)


def _ada_kernel(c_ref, w_ref, b_ref, o_ref):
    c = c_ref[...]
    c_act = (c * _sigmoid(c)).astype(BF16)
    o_ref[0] = _dot(c_act, w_ref[0].astype(BF16)) + b_ref[0]


def _ada(c, w_ada, b_ada):
    depth, d, n = w_ada.shape
    tn = min(512, n)
    c8 = jnp.broadcast_to(c, (SUBLANES, d))
    out = pl.pallas_call(
        _ada_kernel,
        grid=(depth, n // tn),
        in_specs=[pl.BlockSpec((SUBLANES, d), lambda l, j: (0, 0)),
                  pl.BlockSpec((1, d, tn), lambda l, j: (l, 0, j)),
                  pl.BlockSpec((1, 1, tn), lambda l, j: (l, 0, j))],
        out_specs=pl.BlockSpec((1, SUBLANES, tn), lambda l, j: (l, 0, j)),
        out_shape=jax.ShapeDtypeStruct((depth, SUBLANES, n), F32),
        compiler_params=_params(2),
        name="ada_mod",
    )(c8, w_ada, b_ada.reshape(depth, 1, n))
    return out[:, 0, :]


def _rms(x):
    return x * lax.rsqrt(jnp.mean(x * x, axis=-1, keepdims=True) + EPS)


def _norm_mod_kernel(x_ref, g_ref, sc_ref, sh_ref, o_ref):
    y = _rms(x_ref[...]) * g_ref[...]
    o_ref[...] = (y * (1.0 + sc_ref[...]) + sh_ref[...]).astype(o_ref.dtype)


def _norm_mod(x, g, scale, shift):
    t, d = x.shape
    tm = min(512, t)
    row = pl.BlockSpec((1, d), lambda i: (0, 0))
    return pl.pallas_call(
        _norm_mod_kernel,
        grid=(t // tm,),
        in_specs=[pl.BlockSpec((tm, d), lambda i: (i, 0)), row, row, row],
        out_specs=pl.BlockSpec((tm, d), lambda i: (i, 0)),
        out_shape=jax.ShapeDtypeStruct((t, d), BF16),
        compiler_params=_params(1),
        name="norm_mod",
    )(x, g, scale, shift)


def _final_norm_kernel(x_ref, g_ref, o_ref):
    o_ref[...] = _rms(x_ref[...]) * g_ref[...]


def _final_norm(x, g):
    t, d = x.shape
    tm = min(512, t)
    return pl.pallas_call(
        _final_norm_kernel,
        grid=(t // tm,),
        in_specs=[pl.BlockSpec((tm, d), lambda i: (i, 0)), pl.BlockSpec((1, d), lambda i: (0, 0))],
        out_specs=pl.BlockSpec((tm, d), lambda i: (i, 0)),
        out_shape=jax.ShapeDtypeStruct((t, d), F32),
        compiler_params=_params(1),
        name="final_norm",
    )(x, g)


def _rope_table_kernel(pos_ref, f_ref, c_ref, sa_ref, sb_ref):
    ang = pos_ref[...] * f_ref[...]
    cos, sin = jnp.cos(ang), jnp.sin(ang)
    lane = lax.broadcasted_iota(jnp.int32, ang.shape, 1)
    c_ref[...] = jnp.where(lane < ROT_DIM, cos, 1.0)
    sa_ref[...] = jnp.where(lane < ROT_HALF, -sin, 0.0)
    sb_ref[...] = jnp.where((lane >= ROT_HALF) & (lane < ROT_DIM), sin, 0.0)


def _rope_tables(positions):
    t = positions.shape[-1]
    tm = min(1024, t)
    inv_freq = ROPE_THETA ** (-jnp.arange(0, ROT_DIM, 2, dtype=F32) / ROT_DIM)
    f_row = jnp.concatenate([inv_freq, inv_freq, jnp.zeros((LANES - ROT_DIM,), F32)])[None, :]
    pos = positions.astype(F32).reshape(t, 1)
    tab = jax.ShapeDtypeStruct((t, LANES), F32)
    blk = pl.BlockSpec((tm, LANES), lambda i: (i, 0))
    return pl.pallas_call(
        _rope_table_kernel,
        grid=(t // tm,),
        in_specs=[pl.BlockSpec((tm, 1), lambda i: (i, 0)), pl.BlockSpec((1, LANES), lambda i: (0, 0))],
        out_specs=[blk, blk, blk],
        out_shape=[tab, tab, tab],
        compiler_params=_params(1),
        name="rope_tables",
    )(pos, f_row)


def _proj_sconv_kernel(h_ref, wb_ref, wc_ref, wx_ref, cw_ref, o_ref, ext_ref):
    tm = h_ref.shape[0]

    @pl.when(pl.program_id(1) == 0)
    def _():
        ext_ref[0:SUBLANES, :] = jnp.zeros((SUBLANES, ext_ref.shape[1]), F32)

    h = h_ref[...]
    p = _dot(h, wc_ref[...]) * _dot(h, wx_ref[...])
    ext_ref[SUBLANES:, :] = p
    conv = (cw_ref[0:1, :] * ext_ref[pl.ds(SUBLANES - 2, tm), :]
            + cw_ref[1:2, :] * ext_ref[pl.ds(SUBLANES - 1, tm), :]
            + cw_ref[2:3, :] * p)
    o_ref[...] = (_dot(h, wb_ref[...]) * conv).astype(o_ref.dtype)
    ext_ref[0:SUBLANES, :] = ext_ref[tm:tm + SUBLANES, :]


def _proj_sconv(h, w_in, conv_w, col0):
    t, d = h.shape
    tm, tn = min(1024, t), 256
    nb = SC_WIDTH // tn
    c0 = col0 // tn

    def wspec(sec):
        return pl.BlockSpec((d, tn), lambda j, i: (0, c0 + sec * nb + j))

    return pl.pallas_call(
        _proj_sconv_kernel,
        grid=(nb, t // tm),
        in_specs=[pl.BlockSpec((tm, d), lambda j, i: (i, 0)), wspec(0), wspec(1), wspec(2),
                  pl.BlockSpec((SC_KERNEL, tn), lambda j, i: (0, j))],
        out_specs=pl.BlockSpec((tm, tn), lambda j, i: (i, j)),
        out_shape=jax.ShapeDtypeStruct((t, SC_WIDTH), BF16),
        scratch_shapes=[pltpu.VMEM((tm + SUBLANES, tn), F32)],
        compiler_params=_params(2),
        name="proj_sconv",
    )(h, w_in, w_in, w_in, conv_w)


def _proj_glu_kernel(h_ref, wa_ref, wg_ref, o_ref):
    h = h_ref[...]
    o_ref[...] = _dot(h, wa_ref[...]) * _sigmoid(_dot(h, wg_ref[...]))


def _proj_glu(h, w_in, col0):
    t, d = h.shape
    tm, tn = min(1024, t), 512
    nb = CF_WIDTH // tn
    c0 = col0 // tn
    return pl.pallas_call(
        _proj_glu_kernel,
        grid=(nb, t // tm),
        in_specs=[pl.BlockSpec((tm, d), lambda j, i: (i, 0)),
                  pl.BlockSpec((d, tn), lambda j, i: (0, c0 + j)),
                  pl.BlockSpec((d, tn), lambda j, i: (0, c0 + nb + j))],
        out_specs=pl.BlockSpec((tm, tn), lambda j, i: (i, j)),
        out_shape=jax.ShapeDtypeStruct((t, CF_WIDTH), F32),
        compiler_params=_params(2),
        name="proj_glu",
    )(h, w_in, w_in)


def _proj_rope_kernel(h_ref, w_ref, c_ref, sa_ref, sb_ref, o_ref):
    acc = _dot(h_ref[...], w_ref[...])
    tn = acc.shape[1]
    reps = tn // LANES
    c = jnp.tile(c_ref[...], (1, reps))
    sa = jnp.tile(sa_ref[...], (1, reps))
    sb = jnp.tile(sb_ref[...], (1, reps))
    out = acc * c + pltpu.roll(acc, tn - ROT_HALF, 1) * sa + pltpu.roll(acc, ROT_HALF, 1) * sb
    o_ref[...] = out.astype(o_ref.dtype)


def _proj_rope(h, w_in, tables, col0, width):
    t, d = h.shape
    tm, tn = min(1024, t), 512
    c0 = col0 // tn
    tab = pl.BlockSpec((tm, LANES), lambda j, i: (i, 0))
    return pl.pallas_call(
        _proj_rope_kernel,
        grid=(width // tn, t // tm),
        in_specs=[pl.BlockSpec((tm, d), lambda j, i: (i, 0)),
                  pl.BlockSpec((d, tn), lambda j, i: (0, c0 + j)), tab, tab, tab],
        out_specs=pl.BlockSpec((tm, tn), lambda j, i: (i, j)),
        out_shape=jax.ShapeDtypeStruct((t, width), BF16),
        compiler_params=_params(2),
        name="proj_rope",
    )(h, w_in, *tables)


def _proj_act_kernel(h_ref, w_ref, o_ref, *, act):
    acc = _dot(h_ref[...], w_ref[...])
    if act == "sigmoid":
        acc = _sigmoid(acc)
    elif act == "relu2":
        acc = jnp.square(jnp.maximum(acc, 0.0))
    o_ref[...] = acc.astype(o_ref.dtype)


def _proj_act(h, w, col0, width, act, name):
    t, d = h.shape
    tm, tn = min(1024, t), 512
    c0 = col0 // tn
    return pl.pallas_call(
        functools.partial(_proj_act_kernel, act=act),
        grid=(width // tn, t // tm),
        in_specs=[pl.BlockSpec((tm, d), lambda j, i: (i, 0)),
                  pl.BlockSpec((d, tn), lambda j, i: (0, c0 + j))],
        out_specs=pl.BlockSpec((tm, tn), lambda j, i: (i, j)),
        out_shape=jax.ShapeDtypeStruct((t, width), BF16),
        compiler_params=_params(2),
        name=name,
    )(h, w)


CF_ROWS = 64
CF_COLS = 256


def _conformer_kernel(uh_ref, u_ref, w_ref, b_ref, g_ref, be_ref, o_ref, ext_ref, y_ref):
    tm, width = u_ref.shape
    halo = jnp.where(pl.program_id(0) > 0, uh_ref[...], 0.0)
    ext_ref[0:CF_HALO, :] = halo
    ext_ref[CF_HALO:, :] = u_ref[...]
    first = CF_HALO - (CF_KERNEL - 1)
    for r0 in range(0, tm, CF_ROWS):
        for c0 in range(0, width, CF_COLS):
            cs = slice(c0, c0 + CF_COLS)
            acc = jnp.broadcast_to(b_ref[:, cs], (CF_ROWS, CF_COLS))
            for k in range(CF_KERNEL):
                acc = acc + w_ref[k:k + 1, cs] * ext_ref[pl.ds(r0 + first + k, CF_ROWS), cs]
            y_ref[r0:r0 + CF_ROWS, cs] = acc
    y = y_ref[...]
    mu = jnp.mean(y, axis=-1, keepdims=True)
    yc = y - mu
    var = jnp.mean(yc * yc, axis=-1, keepdims=True)
    z = yc * lax.rsqrt(var + EPS) * g_ref[...] + be_ref[...]
    o_ref[...] = (z * _sigmoid(z)).astype(o_ref.dtype)


def _conformer(u, conv_w, bias, ln_g, ln_b):
    t, width = u.shape
    tm = min(128, t)
    per = tm // CF_HALO
    row = pl.BlockSpec((1, width), lambda i: (0, 0))
    return pl.pallas_call(
        _conformer_kernel,
        grid=(t // tm,),
        in_specs=[pl.BlockSpec((CF_HALO, width), lambda i: (jnp.maximum(i * per - 1, 0), 0)),
                  pl.BlockSpec((tm, width), lambda i: (i, 0)),
                  pl.BlockSpec((CF_KERNEL, width), lambda i: (0, 0)), row, row, row],
        out_specs=pl.BlockSpec((tm, width), lambda i: (i, 0)),
        out_shape=jax.ShapeDtypeStruct((t, width), BF16),
        scratch_shapes=[pltpu.VMEM((tm + CF_HALO, width), F32), pltpu.VMEM((tm, width), F32)],
        compiler_params=_params(1),
        name="conformer_conv",
    )(u, u, conv_w, bias, ln_g, ln_b)


def _attn_kernel(q_ref, kp_ref, kc_ref, vp_ref, vc_ref, o_ref, l_ref):
    n = ATTN_BLOCK
    q = q_ref[...]
    k = jnp.concatenate([kp_ref[...], kc_ref[...]], axis=0)
    v = jnp.concatenate([vp_ref[...], vc_ref[...]], axis=0)
    qi = lax.broadcasted_iota(jnp.int32, (n, 2 * n), 0)
    kj = lax.broadcasted_iota(jnp.int32, (n, 2 * n), 1)
    has_prev = pl.program_id(1) > 0
    valid = (kj >= qi) & (kj <= qi + n) & ((kj >= n) | has_prev)
    l_ref[...] = jnp.zeros(l_ref.shape, F32)
    for h in range(DW_HEADS):
        sl = slice(h * HEAD_DIM, (h + 1) * HEAD_DIM)
        s = lax.dot_general(q[:, sl], k[:, sl], (((1,), (1,)), ((), ())),
                            preferred_element_type=F32) * (HEAD_DIM ** -0.5)
        s = jnp.where(valid, s, -jnp.inf)
        m = jnp.max(s, axis=-1, keepdims=True)
        p = jnp.exp(s - m)
        den = jnp.sum(p, axis=-1, keepdims=True)
        o_ref[:, sl] = _dot((p / den).astype(BF16), v[:, sl])
        l_ref[:, h:h + 1] = m + jnp.log(den)


def _attention_group(qk, v, group, dilation):
    t = v.shape[0]
    n = ATTN_BLOCK
    sub = t // dilation
    assert sub % n == 0
    gw = DW_GROUP_WIDTH
    qkv = qk.reshape(sub, dilation * 2 * DW_WIDTH)
    vv = v.reshape(sub, dilation * DW_WIDTH)

    def cur(sec, n_sec):
        return lambda r, b: (b, (r * n_sec + sec) * N_DW_GROUPS + group)

    def prev(sec, n_sec):
        return lambda r, b: (jnp.maximum(b - 1, 0), (r * n_sec + sec) * N_DW_GROUPS + group)

    o, lse = pl.pallas_call(
        _attn_kernel,
        grid=(dilation, sub // n),
        in_specs=[pl.BlockSpec((n, gw), cur(0, 2)), pl.BlockSpec((n, gw), prev(1, 2)),
                  pl.BlockSpec((n, gw), cur(1, 2)), pl.BlockSpec((n, gw), prev(0, 1)),
                  pl.BlockSpec((n, gw), cur(0, 1))],
        out_specs=[pl.BlockSpec((n, gw), lambda r, b: (b, r)),
                   pl.BlockSpec((n, LANES), lambda r, b: (b, r))],
        out_shape=[jax.ShapeDtypeStruct((sub, dilation * gw), F32),
                   jax.ShapeDtypeStruct((sub, dilation * LANES), F32)],
        compiler_params=_params(2),
        name=f"dilated_attn_d{dilation}",
    )(qkv, qkv, qkv, vv, vv)
    return o.reshape(t, gw), lse.reshape(t, LANES)


def _combine_kernel(o0_ref, o1_ref, o2_ref, l0_ref, l1_ref, l2_ref, out_ref):
    o_refs = (o0_ref, o1_ref, o2_ref)
    l0, l1, l2 = l0_ref[...], l1_ref[...], l2_ref[...]
    m = jnp.maximum(jnp.maximum(l0, l1), l2)
    e = [jnp.exp(l - m) for l in (l0, l1, l2)]
    den = e[0] + e[1] + e[2]
    w = [x / den for x in e]
    for h in range(DW_HEADS):
        sl = slice(h * HEAD_DIM, (h + 1) * HEAD_DIM)
        acc = w[0][:, h:h + 1] * o_refs[0][:, sl]
        for g in (1, 2):
            acc = acc + w[g][:, h:h + 1] * o_refs[g][:, sl]
        out_ref[:, sl] = acc.astype(out_ref.dtype)


def _combine(outs, lses):
    t, gw = outs[0].shape
    tm = min(512, t)
    ob = pl.BlockSpec((tm, gw), lambda i: (i, 0))
    lb = pl.BlockSpec((tm, LANES), lambda i: (i, 0))
    return pl.pallas_call(
        _combine_kernel,
        grid=(t // tm,),
        in_specs=[ob, ob, ob, lb, lb, lb],
        out_specs=ob,
        out_shape=jax.ShapeDtypeStruct((t, gw), BF16),
        compiler_params=_params(1),
        name="attn_combine",
    )(*outs, *lses)


def _merge_kernel(za_ref, zb_ref, zc_ref, wa_ref, wb_ref, wc_ref, ga_ref, gb_ref, gc_ref, o_ref):
    m = ga_ref[...].astype(F32) * _dot(za_ref[...], wa_ref[...])
    m = m + gb_ref[...].astype(F32) * _dot(zb_ref[...], wb_ref[...])
    m = m + gc_ref[...].astype(F32) * _dot(zc_ref[...], wc_ref[...])
    o_ref[...] = m.astype(o_ref.dtype)


def _merge(za, zb, zc, wa, wb, wc, gates):
    t = za.shape[0]
    d = wa.shape[1]
    tm, tn = min(512, t), min(512, d)
    nb = d // tn

    def act(a):
        return pl.BlockSpec((tm, a.shape[1]), lambda j, i: (i, 0))

    def wgt(a):
        return pl.BlockSpec((a.shape[0], tn), lambda j, i: (0, j))

    def gate(sec):
        return pl.BlockSpec((tm, tn), lambda j, i: (i, sec * nb + j))

    return pl.pallas_call(
        _merge_kernel,
        grid=(nb, t // tm),
        in_specs=[act(za), act(zb), act(zc), wgt(wa), wgt(wb), wgt(wc), gate(0), gate(1), gate(2)],
        out_specs=pl.BlockSpec((tm, tn), lambda j, i: (i, j)),
        out_shape=jax.ShapeDtypeStruct((t, d), BF16),
        compiler_params=_params(2),
        name="merge_branches",
    )(za, zb, zc, wa, wb, wc, gates, gates, gates)


def _resid_kernel(a_ref, w_ref, x_ref, g_ref, o_ref, acc_ref):
    kk = pl.program_id(2)

    @pl.when(kk == 0)
    def _():
        acc_ref[...] = jnp.zeros(acc_ref.shape, F32)

    acc_ref[...] += _dot(a_ref[...], w_ref[...])

    @pl.when(kk == pl.num_programs(2) - 1)
    def _():
        o_ref[...] = x_ref[...] + g_ref[...] * acc_ref[...]


def _resid_matmul(a, w, x, gate, name):
    t, kdim = a.shape
    d = w.shape[1]
    tm, tn, tk = min(1024, t), min(1024, d), min(2048, kdim)
    return pl.pallas_call(
        _resid_kernel,
        grid=(d // tn, t // tm, kdim // tk),
        in_specs=[pl.BlockSpec((tm, tk), lambda j, i, k: (i, k)),
                  pl.BlockSpec((tk, tn), lambda j, i, k: (k, j)),
                  pl.BlockSpec((tm, tn), lambda j, i, k: (i, j)),
                  pl.BlockSpec((1, tn), lambda j, i, k: (0, j))],
        out_specs=pl.BlockSpec((tm, tn), lambda j, i, k: (i, j)),
        out_shape=jax.ShapeDtypeStruct((t, d), F32),
        scratch_shapes=[pltpu.VMEM((tm, tn), F32)],
        compiler_params=_params(3),
        name=name,
    )(a, w, x, gate)


def kernel(x, c, positions, w_ada, b_ada, g_mix, w_in, conv_a, conv_b, conv_b_bias, ln_cf_g, ln_cf_b,
           w_out_a, w_out_b, w_out_c, w_o, g_mlp, w_mlp1, w_mlp2, g_final):
    b, t, d = x.shape
    assert b == 1, "kernel written for a single sequence"
    depth = w_ada.shape[0]
    d_ff = w_mlp1.shape[-1]
    xs = x[0]

    mod = _ada(c, w_ada, b_ada).reshape(depth, N_MOD, 1, d)
    tables = _rope_tables(positions[0])

    col_sc = 0
    col_cf = col_sc + 3 * SC_WIDTH
    col_q = col_cf + 2 * CF_WIDTH
    col_v = col_q + 2 * DW_WIDTH
    col_gate = col_v + DW_WIDTH

    for l in range(depth):
        shift1, scale1, gate1, shift2, scale2, gate2 = (mod[l, i] for i in range(N_MOD))
        w_in_l = w_in[l].astype(BF16)

        h = _norm_mod(xs, g_mix[l][None, :], scale1, shift1)
        z_a = _proj_sconv(h, w_in_l, conv_a[l], col_sc)
        u = _proj_glu(h, w_in_l, col_cf)
        qk = _proj_rope(h, w_in_l, tables, col_q, 2 * DW_WIDTH)
        v = _proj_act(h, w_in_l, col_v, DW_WIDTH, "none", "proj_v")
        gates = _proj_act(h, w_in_l, col_gate, 3 * d, "sigmoid", "proj_gates")

        z_b = _conformer(u, conv_b[l], conv_b_bias[l][None, :], ln_cf_g[l][None, :], ln_cf_b[l][None, :])

        outs, lses = [], []
        for g, (window, dilation) in enumerate(DW_PATTERNS):
            assert window // dilation == ATTN_BLOCK
            o_g, lse_g = _attention_group(qk, v, g, dilation)
            outs.append(o_g)
            lses.append(lse_g)
        z_c = _combine(outs, lses)

        merged = _merge(z_a, z_b, z_c, w_out_a[l].astype(BF16), w_out_b[l].astype(BF16),
                        w_out_c[l].astype(BF16), gates)
        xs = _resid_matmul(merged, w_o[l].astype(BF16), xs, gate1, "out_proj_resid")

        h2 = _norm_mod(xs, g_mlp[l][None, :], scale2, shift2)
        a = _proj_act(h2, w_mlp1[l].astype(BF16), 0, d_ff, "relu2", "mlp_up")
        xs = _resid_matmul(a, w_mlp2[l].astype(BF16), xs, gate2, "mlp_down_resid")

    return _final_norm(xs, g_final[None, :])[None]
```

```python
import functools

import jax
import jax.numpy as jnp
from jax import lax
from jax.experimental import pallas as pl
from jax.experimental.pallas import tpu as pltpu

F32 = jnp.float32
BF16 = jnp.bfloat16

EPS = 1e-6
SC_WIDTH = 2048
SC_KERNEL = 3
CF_WIDTH = 2048
CF_KERNEL = 31
HEAD_DIM = 128
DW_PATTERNS = ((128, 1), (512, 4), (2048, 16))
N_DW_GROUPS = len(DW_PATTERNS)
DW_HEADS = 8
DW_GROUP_WIDTH = DW_HEADS * HEAD_DIM
DW_WIDTH = N_DW_GROUPS * DW_GROUP_WIDTH
ATTN_BLOCK = 128
ROT_DIM = HEAD_DIM // 4
ROT_HALF = ROT_DIM // 2
ROPE_THETA = 500000.0
N_MOD = 6

LANES = 128
SUBLANES = 8
CF_HALO = 32
VMEM_LIMIT = 56 * 1024 * 1024


def _params(n_axes):
    return pltpu.CompilerParams(dimension_semantics=("arbitrary",) * n_axes,
                                vmem_limit_bytes=VMEM_LIMIT)


def _dot(a, b):
    return jnp.dot(a, b, preferred_element_type=F32)


def _sigmoid(v):
    return jax.nn.sigmoid(v)


def _cast_resident_weights(*pairs):
    @pl.when(pl.program_id(1) == 0)
    def _():
        for w_ref, wbf_ref in pairs:
            wbf_ref[...] = w_ref[...].astype(BF16)


def _layer_weight_spec(layer, rows, tn, col_block):
    return pl.BlockSpec((None, rows, tn), lambda j, i: (layer, 0, col_block(j)))


def _ada_kernel(c_ref, w_ref, b_ref, o_ref):
    c = c_ref[...]
    c_act = (c * _sigmoid(c)).astype(BF16)
    o_ref[0] = _dot(c_act, w_ref[0].astype(BF16)) + b_ref[0]


def _ada(c, w_ada, b_ada):
    depth, d, n = w_ada.shape
    tn = min(512, n)
    c8 = jnp.broadcast_to(c, (SUBLANES, d))
    out = pl.pallas_call(
        _ada_kernel,
        grid=(depth, n // tn),
        in_specs=[pl.BlockSpec((SUBLANES, d), lambda l, j: (0, 0)),
                  pl.BlockSpec((1, d, tn), lambda l, j: (l, 0, j)),
                  pl.BlockSpec((1, 1, tn), lambda l, j: (l, 0, j))],
        out_specs=pl.BlockSpec((1, SUBLANES, tn), lambda l, j: (l, 0, j)),
        out_shape=jax.ShapeDtypeStruct((depth, SUBLANES, n), F32),
        compiler_params=_params(2),
        name="ada_mod",
    )(c8, w_ada, b_ada.reshape(depth, 1, n))
    return out[:, 0, :]


def _rms(x):
    return x * lax.rsqrt(jnp.mean(x * x, axis=-1, keepdims=True) + EPS)


def _norm_mod_kernel(x_ref, g_ref, sc_ref, sh_ref, o_ref):
    y = _rms(x_ref[...]) * g_ref[...]
    o_ref[...] = (y * (1.0 + sc_ref[...]) + sh_ref[...]).astype(o_ref.dtype)


def _norm_mod(x, g, scale, shift):
    t, d = x.shape
    tm = min(512, t)
    row = pl.BlockSpec((1, d), lambda i: (0, 0))
    return pl.pallas_call(
        _norm_mod_kernel,
        grid=(t // tm,),
        in_specs=[pl.BlockSpec((tm, d), lambda i: (i, 0)), row, row, row],
        out_specs=pl.BlockSpec((tm, d), lambda i: (i, 0)),
        out_shape=jax.ShapeDtypeStruct((t, d), BF16),
        compiler_params=_params(1),
        name="norm_mod",
    )(x, g, scale, shift)


def _final_norm_kernel(x_ref, g_ref, o_ref):
    o_ref[...] = _rms(x_ref[...]) * g_ref[...]


def _final_norm(x, g):
    t, d = x.shape
    tm = min(512, t)
    return pl.pallas_call(
        _final_norm_kernel,
        grid=(t // tm,),
        in_specs=[pl.BlockSpec((tm, d), lambda i: (i, 0)), pl.BlockSpec((1, d), lambda i: (0, 0))],
        out_specs=pl.BlockSpec((tm, d), lambda i: (i, 0)),
        out_shape=jax.ShapeDtypeStruct((t, d), F32),
        compiler_params=_params(1),
        name="final_norm",
    )(x, g)


def _rope_table_kernel(pos_ref, f_ref, c_ref, sa_ref, sb_ref):
    ang = pos_ref[...] * f_ref[...]
    cos, sin = jnp.cos(ang), jnp.sin(ang)
    lane = lax.broadcasted_iota(jnp.int32, ang.shape, 1)
    c_ref[...] = jnp.where(lane < ROT_DIM, cos, 1.0)
    sa_ref[...] = jnp.where(lane < ROT_HALF, -sin, 0.0)
    sb_ref[...] = jnp.where((lane >= ROT_HALF) & (lane < ROT_DIM), sin, 0.0)


def _rope_tables(positions):
    t = positions.shape[-1]
    tm = min(1024, t)
    inv_freq = ROPE_THETA ** (-jnp.arange(0, ROT_DIM, 2, dtype=F32) / ROT_DIM)
    f_row = jnp.concatenate([inv_freq, inv_freq, jnp.zeros((LANES - ROT_DIM,), F32)])[None, :]
    pos = positions.astype(F32).reshape(t, 1)
    tab = jax.ShapeDtypeStruct((t, LANES), F32)
    blk = pl.BlockSpec((tm, LANES), lambda i: (i, 0))
    return pl.pallas_call(
        _rope_table_kernel,
        grid=(t // tm,),
        in_specs=[pl.BlockSpec((tm, 1), lambda i: (i, 0)), pl.BlockSpec((1, LANES), lambda i: (0, 0))],
        out_specs=[blk, blk, blk],
        out_shape=[tab, tab, tab],
        compiler_params=_params(1),
        name="rope_tables",
    )(pos, f_row)


def _proj_sconv_kernel(h_ref, wb_ref, wc_ref, wx_ref, cw_ref, o_ref, wb16, wc16, wx16, ext_ref):
    tm = h_ref.shape[0]
    _cast_resident_weights((wb_ref, wb16), (wc_ref, wc16), (wx_ref, wx16))

    @pl.when(pl.program_id(1) == 0)
    def _():
        ext_ref[0:SUBLANES, :] = jnp.zeros((SUBLANES, ext_ref.shape[1]), F32)

    h = h_ref[...]
    p = _dot(h, wc16[...]) * _dot(h, wx16[...])
    ext_ref[SUBLANES:, :] = p
    conv = (cw_ref[0:1, :] * ext_ref[pl.ds(SUBLANES - 2, tm), :]
            + cw_ref[1:2, :] * ext_ref[pl.ds(SUBLANES - 1, tm), :]
            + cw_ref[2:3, :] * p)
    o_ref[...] = (_dot(h, wb16[...]) * conv).astype(o_ref.dtype)
    ext_ref[0:SUBLANES, :] = ext_ref[tm:tm + SUBLANES, :]


def _proj_sconv(h, w_in, layer, conv_w, col0):
    t, d = h.shape
    tm, tn = min(512, t), 256
    nb = SC_WIDTH // tn
    c0 = col0 // tn

    def wspec(sec):
        return _layer_weight_spec(layer, d, tn, lambda j: c0 + sec * nb + j)

    return pl.pallas_call(
        _proj_sconv_kernel,
        grid=(nb, t // tm),
        in_specs=[pl.BlockSpec((tm, d), lambda j, i: (i, 0)), wspec(0), wspec(1), wspec(2),
                  pl.BlockSpec((SC_KERNEL, tn), lambda j, i: (0, j))],
        out_specs=pl.BlockSpec((tm, tn), lambda j, i: (i, j)),
        out_shape=jax.ShapeDtypeStruct((t, SC_WIDTH), BF16),
        scratch_shapes=[pltpu.VMEM((d, tn), BF16)] * 3 + [pltpu.VMEM((tm + SUBLANES, tn), F32)],
        compiler_params=_params(2),
        name="proj_sconv",
    )(h, w_in, w_in, w_in, conv_w)


def _proj_glu_kernel(h_ref, wa_ref, wg_ref, o_ref, wa16, wg16):
    _cast_resident_weights((wa_ref, wa16), (wg_ref, wg16))
    h = h_ref[...]
    o_ref[...] = _dot(h, wa16[...]) * _sigmoid(_dot(h, wg16[...]))


def _proj_glu(h, w_in, layer, col0):
    t, d = h.shape
    tm, tn = min(1024, t), 256
    nb = CF_WIDTH // tn
    c0 = col0 // tn
    return pl.pallas_call(
        _proj_glu_kernel,
        grid=(nb, t // tm),
        in_specs=[pl.BlockSpec((tm, d), lambda j, i: (i, 0)),
                  _layer_weight_spec(layer, d, tn, lambda j: c0 + j),
                  _layer_weight_spec(layer, d, tn, lambda j: c0 + nb + j)],
        out_specs=pl.BlockSpec((tm, tn), lambda j, i: (i, j)),
        out_shape=jax.ShapeDtypeStruct((t, CF_WIDTH), F32),
        scratch_shapes=[pltpu.VMEM((d, tn), BF16)] * 2,
        compiler_params=_params(2),
        name="proj_glu",
    )(h, w_in, w_in)


def _proj_qkv_kernel(h_ref, w_ref, c_ref, sa_ref, sb_ref, o_ref, w16, slab_ref, *, dilation, tiles_per_sec):
    _cast_resident_weights((w_ref, w16))
    tm = h_ref.shape[0]
    acc = _dot(h_ref[...], w16[...])
    tn = acc.shape[1]
    n_slabs = tn // LANES

    def store(val):
        if dilation == 1:
            o_ref[0] = val.astype(o_ref.dtype)
            return
        for s in range(n_slabs):
            slab_ref[s] = val[:, s * LANES:(s + 1) * LANES]
        rows = tm // dilation
        for r in range(dilation):
            for s in range(n_slabs):
                o_ref[r, :, s * LANES:(s + 1) * LANES] = (
                    slab_ref[s, pl.ds(r, rows, stride=dilation), :].astype(o_ref.dtype))

    is_v = pl.program_id(0) >= 2 * tiles_per_sec

    @pl.when(is_v)
    def _():
        store(acc)

    @pl.when(jnp.logical_not(is_v))
    def _():
        c = jnp.tile(c_ref[...], (1, n_slabs))
        sa = jnp.tile(sa_ref[...], (1, n_slabs))
        sb = jnp.tile(sb_ref[...], (1, n_slabs))
        store(acc * c + pltpu.roll(acc, tn - ROT_HALF, 1) * sa + pltpu.roll(acc, ROT_HALF, 1) * sb)


def _proj_qkv(h, w_in, layer, tables, col_q, group, dilation):
    t, d = h.shape
    tm, tn = min(1024, t), 512
    gw = DW_GROUP_WIDTH
    tps = gw // tn
    c0 = (col_q + group * gw) // tn
    sec_stride = DW_WIDTH // tn
    tab = pl.BlockSpec((tm, LANES), lambda j, i: (i, 0))
    return pl.pallas_call(
        functools.partial(_proj_qkv_kernel, dilation=dilation, tiles_per_sec=tps),
        grid=(3 * tps, t // tm),
        in_specs=[pl.BlockSpec((tm, d), lambda j, i: (i, 0)),
                  _layer_weight_spec(layer, d, tn, lambda j: c0 + (j // tps) * sec_stride + j % tps),
                  tab, tab, tab],
        out_specs=pl.BlockSpec((None, dilation, tm // dilation, tn), lambda j, i: (j // tps, 0, i, j % tps)),
        out_shape=jax.ShapeDtypeStruct((3, dilation, t // dilation, gw), BF16),
        scratch_shapes=[pltpu.VMEM((d, tn), BF16), pltpu.VMEM((tn // LANES, tm, LANES), F32)],
        compiler_params=_params(2),
        name=f"proj_qkv_d{dilation}",
    )(h, w_in, *tables)


def _proj_act_kernel(h_ref, w_ref, o_ref, w16, *, act):
    _cast_resident_weights((w_ref, w16))
    acc = _dot(h_ref[...], w16[...])
    if act == "sigmoid":
        acc = _sigmoid(acc)
    elif act == "relu2":
        acc = jnp.square(jnp.maximum(acc, 0.0))
    o_ref[...] = acc.astype(o_ref.dtype)


def _proj_act(h, w, layer, col0, width, act, name):
    t, d = h.shape
    tm, tn = min(1024, t), 512
    c0 = col0 // tn
    return pl.pallas_call(
        functools.partial(_proj_act_kernel, act=act),
        grid=(width // tn, t // tm),
        in_specs=[pl.BlockSpec((tm, d), lambda j, i: (i, 0)),
                  _layer_weight_spec(layer, d, tn, lambda j: c0 + j)],
        out_specs=pl.BlockSpec((tm, tn), lambda j, i: (i, j)),
        out_shape=jax.ShapeDtypeStruct((t, width), BF16),
        scratch_shapes=[pltpu.VMEM((d, tn), BF16)],
        compiler_params=_params(2),
        name=name,
    )(h, w)


CF_ROWS = 64
CF_COLS = 256


def _conformer_kernel(uh_ref, u_ref, w_ref, b_ref, g_ref, be_ref, o_ref, ext_ref, y_ref):
    tm, width = u_ref.shape
    halo = jnp.where(pl.program_id(0) > 0, uh_ref[...], 0.0)
    ext_ref[0:CF_HALO, :] = halo
    ext_ref[CF_HALO:, :] = u_ref[...]
    first = CF_HALO - (CF_KERNEL - 1)
    for r0 in range(0, tm, CF_ROWS):
        for c0 in range(0, width, CF_COLS):
            cs = slice(c0, c0 + CF_COLS)
            acc = jnp.broadcast_to(b_ref[:, cs], (CF_ROWS, CF_COLS))
            for k in range(CF_KERNEL):
                acc = acc + w_ref[k:k + 1, cs] * ext_ref[pl.ds(r0 + first + k, CF_ROWS), cs]
            y_ref[r0:r0 + CF_ROWS, cs] = acc
    y = y_ref[...]
    mu = jnp.mean(y, axis=-1, keepdims=True)
    yc = y - mu
    var = jnp.mean(yc * yc, axis=-1, keepdims=True)
    z = yc * lax.rsqrt(var + EPS) * g_ref[...] + be_ref[...]
    o_ref[...] = (z * _sigmoid(z)).astype(o_ref.dtype)


def _conformer(u, conv_w, bias, ln_g, ln_b):
    t, width = u.shape
    tm = min(128, t)
    per = tm // CF_HALO
    row = pl.BlockSpec((1, width), lambda i: (0, 0))
    return pl.pallas_call(
        _conformer_kernel,
        grid=(t // tm,),
        in_specs=[pl.BlockSpec((CF_HALO, width), lambda i: (jnp.maximum(i * per - 1, 0), 0)),
                  pl.BlockSpec((tm, width), lambda i: (i, 0)),
                  pl.BlockSpec((CF_KERNEL, width), lambda i: (0, 0)), row, row, row],
        out_specs=pl.BlockSpec((tm, width), lambda i: (i, 0)),
        out_shape=jax.ShapeDtypeStruct((t, width), BF16),
        scratch_shapes=[pltpu.VMEM((tm + CF_HALO, width), F32), pltpu.VMEM((tm, width), F32)],
        compiler_params=_params(1),
        name="conformer_conv",
    )(u, u, conv_w, bias, ln_g, ln_b)


def _attn_kernel(q_ref, kp_ref, kc_ref, vp_ref, vc_ref, o_ref, l_ref):
    n = ATTN_BLOCK
    q = q_ref[...]
    k = jnp.concatenate([kp_ref[...], kc_ref[...]], axis=0)
    v = jnp.concatenate([vp_ref[...], vc_ref[...]], axis=0)
    qi = lax.broadcasted_iota(jnp.int32, (n, 2 * n), 0)
    kj = lax.broadcasted_iota(jnp.int32, (n, 2 * n), 1)
    has_prev = pl.program_id(1) > 0
    valid = (kj >= qi) & (kj <= qi + n) & ((kj >= n) | has_prev)
    l_ref[...] = jnp.zeros(l_ref.shape, F32)
    for h in range(DW_HEADS):
        sl = slice(h * HEAD_DIM, (h + 1) * HEAD_DIM)
        s = lax.dot_general(q[:, sl], k[:, sl], (((1,), (1,)), ((), ())),
                            preferred_element_type=F32) * (HEAD_DIM ** -0.5)
        s = jnp.where(valid, s, -jnp.inf)
        m = jnp.max(s, axis=-1, keepdims=True)
        p = jnp.exp(s - m)
        den = jnp.sum(p, axis=-1, keepdims=True)
        o_ref[:, sl] = _dot((p / den).astype(BF16), v[:, sl])
        l_ref[:, h:h + 1] = m + jnp.log(den)


def _attention_group(qkv):
    _, dilation, sub, gw = qkv.shape
    n = ATTN_BLOCK
    assert sub % n == 0

    def cur(sec):
        return pl.BlockSpec((None, None, n, gw), lambda r, b: (sec, r, b, 0))

    def prev(sec):
        return pl.BlockSpec((None, None, n, gw), lambda r, b: (sec, r, jnp.maximum(b - 1, 0), 0))

    return pl.pallas_call(
        _attn_kernel,
        grid=(dilation, sub // n),
        in_specs=[cur(0), prev(1), cur(1), prev(2), cur(2)],
        out_specs=[pl.BlockSpec((None, n, gw), lambda r, b: (r, b, 0)),
                   pl.BlockSpec((None, n, LANES), lambda r, b: (r, b, 0))],
        out_shape=[jax.ShapeDtypeStruct((dilation, sub, gw), F32),
                   jax.ShapeDtypeStruct((dilation, sub, LANES), F32)],
        compiler_params=_params(2),
        name=f"dilated_attn_d{dilation}",
    )(qkv, qkv, qkv, qkv, qkv)


def _combine_kernel(o0_ref, o1_ref, o2_ref, l0_ref, l1_ref, l2_ref, out_ref, lnat_ref, onat_ref):
    o_refs = (o0_ref, o1_ref, o2_ref)
    l_refs = (l0_ref, l1_ref, l2_ref)
    tm = out_ref.shape[0]

    def natural(dst_ref, src, dilation):
        rows = tm // dilation
        for r in range(dilation):
            dst_ref[pl.ds(r, rows, stride=dilation), :] = src(r)

    lse = []
    for g, l_ref in enumerate(l_refs):
        dilation = l_ref.shape[0]
        if dilation == 1:
            lse.append(l_ref[0])
        else:
            natural(lnat_ref.at[g], lambda r, l_ref=l_ref: l_ref[r], dilation)
            lse.append(lnat_ref[g])
    m = jnp.maximum(jnp.maximum(lse[0], lse[1]), lse[2])
    e = [jnp.exp(l - m) for l in lse]
    den = e[0] + e[1] + e[2]
    w = [x / den for x in e]
    for h in range(DW_HEADS):
        sl = slice(h * HEAD_DIM, (h + 1) * HEAD_DIM)
        acc = None
        for g, o_ref in enumerate(o_refs):
            dilation = o_ref.shape[0]
            if dilation == 1:
                o_nat = o_ref[0, :, sl]
            else:
                natural(onat_ref, lambda r, o_ref=o_ref: o_ref[r, :, sl], dilation)
                o_nat = onat_ref[...]
            term = w[g][:, h:h + 1] * o_nat
            acc = term if acc is None else acc + term
        out_ref[:, sl] = acc.astype(out_ref.dtype)


def _combine(outs, lses):
    gw = outs[0].shape[-1]
    t = outs[0].shape[0] * outs[0].shape[1]
    tm = min(512, t)

    def spec(a):
        dil = a.shape[0]
        return pl.BlockSpec((dil, tm // dil, a.shape[-1]), lambda i: (0, i, 0))

    return pl.pallas_call(
        _combine_kernel,
        grid=(t // tm,),
        in_specs=[spec(a) for a in outs] + [spec(a) for a in lses],
        out_specs=pl.BlockSpec((tm, gw), lambda i: (i, 0)),
        out_shape=jax.ShapeDtypeStruct((t, gw), BF16),
        scratch_shapes=[pltpu.VMEM((N_DW_GROUPS, tm, LANES), F32), pltpu.VMEM((tm, LANES), F32)],
        compiler_params=_params(1),
        name="attn_combine",
    )(*outs, *lses)


def _merge_kernel(za_ref, zb_ref, zc_ref, wa_ref, wb_ref, wc_ref, ga_ref, gb_ref, gc_ref, o_ref,
                  wa16, wb16, wc16):
    _cast_resident_weights((wa_ref, wa16), (wb_ref, wb16), (wc_ref, wc16))
    m = ga_ref[...].astype(F32) * _dot(za_ref[...], wa16[...])
    m = m + gb_ref[...].astype(F32) * _dot(zb_ref[...], wb16[...])
    m = m + gc_ref[...].astype(F32) * _dot(zc_ref[...], wc16[...])
    o_ref[...] = m.astype(o_ref.dtype)


def _merge(za, zb, zc, wa, wb, wc, layer, gates):
    t = za.shape[0]
    d = wa.shape[-1]
    tm, tn = min(512, t), min(512, d)
    nb = d // tn

    def act(a):
        return pl.BlockSpec((tm, a.shape[1]), lambda j, i: (i, 0))

    def wgt(a):
        return _layer_weight_spec(layer, a.shape[1], tn, lambda j: j)

    def gate(sec):
        return pl.BlockSpec((tm, tn), lambda j, i: (i, sec * nb + j))

    return pl.pallas_call(
        _merge_kernel,
        grid=(nb, t // tm),
        in_specs=[act(za), act(zb), act(zc), wgt(wa), wgt(wb), wgt(wc), gate(0), gate(1), gate(2)],
        out_specs=pl.BlockSpec((tm, tn), lambda j, i: (i, j)),
        out_shape=jax.ShapeDtypeStruct((t, d), BF16),
        scratch_shapes=[pltpu.VMEM((a.shape[1], tn), BF16) for a in (wa, wb, wc)],
        compiler_params=_params(2),
        name="merge_branches",
    )(za, zb, zc, wa, wb, wc, gates, gates, gates)


def _resid_kernel(a_ref, w_ref, x_ref, g_ref, o_ref, acc_ref):
    kk = pl.program_id(2)

    @pl.when(kk == 0)
    def _():
        acc_ref[...] = jnp.zeros(acc_ref.shape, F32)

    acc_ref[...] += _dot(a_ref[...], w_ref[...])

    @pl.when(kk == pl.num_programs(2) - 1)
    def _():
        o_ref[...] = x_ref[...] + g_ref[...] * acc_ref[...]


def _resid_matmul(a, w, x, gate, name):
    t, kdim = a.shape
    d = w.shape[1]
    tm, tn, tk = min(1024, t), min(1024, d), min(2048, kdim)
    return pl.pallas_call(
        _resid_kernel,
        grid=(d // tn, t // tm, kdim // tk),
        in_specs=[pl.BlockSpec((tm, tk), lambda j, i, k: (i, k)),
                  pl.BlockSpec((tk, tn), lambda j, i, k: (k, j)),
                  pl.BlockSpec((tm, tn), lambda j, i, k: (i, j)),
                  pl.BlockSpec((1, tn), lambda j, i, k: (0, j))],
        out_specs=pl.BlockSpec((tm, tn), lambda j, i, k: (i, j)),
        out_shape=jax.ShapeDtypeStruct((t, d), F32),
        scratch_shapes=[pltpu.VMEM((tm, tn), F32)],
        compiler_params=_params(3),
        name=name,
    )(a, w, x, gate)


def kernel(x, c, positions, w_ada, b_ada, g_mix, w_in, conv_a, conv_b, conv_b_bias, ln_cf_g, ln_cf_b,
           w_out_a, w_out_b, w_out_c, w_o, g_mlp, w_mlp1, w_mlp2, g_final):
    b, t, d = x.shape
    assert b == 1, "kernel written for a single sequence"
    depth = w_ada.shape[0]
    d_ff = w_mlp1.shape[-1]
    xs = x[0]

    mod = _ada(c, w_ada, b_ada).reshape(depth, N_MOD, 1, d)
    tables = _rope_tables(positions[0])

    col_sc = 0
    col_cf = col_sc + 3 * SC_WIDTH
    col_q = col_cf + 2 * CF_WIDTH
    col_gate = col_q + 3 * DW_WIDTH

    for l in range(depth):
        shift1, scale1, gate1, shift2, scale2, gate2 = (mod[l, i] for i in range(N_MOD))

        h = _norm_mod(xs, g_mix[l][None, :], scale1, shift1)
        z_a = _proj_sconv(h, w_in, l, conv_a[l], col_sc)
        u = _proj_glu(h, w_in, l, col_cf)
        z_b = _conformer(u, conv_b[l], conv_b_bias[l][None, :], ln_cf_g[l][None, :], ln_cf_b[l][None, :])

        outs, lses = [], []
        for g, (window, dilation) in enumerate(DW_PATTERNS):
            assert window // dilation == ATTN_BLOCK
            o_g, lse_g = _attention_group(_proj_qkv(h, w_in, l, tables, col_q, g, dilation))
            outs.append(o_g)
            lses.append(lse_g)
        z_c = _combine(outs, lses)

        gates = _proj_act(h, w_in, l, col_gate, 3 * d, "sigmoid", "proj_gates")
        merged = _merge(z_a, z_b, z_c, w_out_a, w_out_b, w_out_c, l, gates)
        xs = _resid_matmul(merged, w_o[l].astype(BF16), xs, gate1, "out_proj_resid")

        h2 = _norm_mod(xs, g_mlp[l][None, :], scale2, shift2)
        a = _proj_act(h2, w_mlp1, l, 0, d_ff, "relu2", "mlp_up")
        xs = _resid_matmul(a, w_mlp2[l].astype(BF16), xs, gate2, "mlp_down_resid")

    return _final_norm(xs, g_final[None, :])[None]
```

```python
import functools

import jax
import jax.numpy as jnp
from jax import lax
from jax.experimental import pallas as pl
from jax.experimental.pallas import tpu as pltpu

F32 = jnp.float32
BF16 = jnp.bfloat16

EPS = 1e-6
SC_WIDTH = 2048
SC_KERNEL = 3
CF_WIDTH = 2048
CF_KERNEL = 31
HEAD_DIM = 128
DW_PATTERNS = ((128, 1), (512, 4), (2048, 16))
N_DW_GROUPS = len(DW_PATTERNS)
DW_HEADS = 8
DW_GROUP_WIDTH = DW_HEADS * HEAD_DIM
DW_WIDTH = N_DW_GROUPS * DW_GROUP_WIDTH
ATTN_BLOCK = 128
ROT_DIM = HEAD_DIM // 4
ROT_HALF = ROT_DIM // 2
ROPE_THETA = 500000.0
N_MOD = 6

LANES = 128
SUBLANES = 8
MXU_COLS = 256
CF_HALO = 32
VMEM_LIMIT = 58 * 1024 * 1024


def _params(n_axes):
    return pltpu.CompilerParams(dimension_semantics=("arbitrary",) * n_axes,
                                vmem_limit_bytes=VMEM_LIMIT)


def _dot(a, b):
    return jnp.dot(a, b, preferred_element_type=F32)


def _sigmoid(v):
    return jax.nn.sigmoid(v)


def _stream_weight_spec(layer, k_rows, n_chunks, tn, n_col_tiles, col_block):
    chunk = k_rows // n_chunks
    assert chunk * n_chunks == k_rows and chunk % 16 == 0

    def imap(jj, i):
        return (layer, jnp.where(jj < n_col_tiles, i, 0), col_block(jnp.minimum(jj, n_col_tiles - 1)))

    return pl.BlockSpec((None, chunk, tn), imap)


def _row_spec(tm, width):
    return pl.BlockSpec((tm, width), lambda jj, i: (jnp.where(jj == 0, 0, i), 0))


def _tile_spec(tm, tn, col_block=lambda j: j):
    return pl.BlockSpec((tm, tn), lambda jj, i: (jnp.where(jj == 0, 0, i), col_block(jnp.maximum(jj - 1, 0))))


def _col_spec(rows, tn):
    return pl.BlockSpec((rows, tn), lambda jj, i: (0, jnp.maximum(jj - 1, 0)))


def _stream_cast(pairs):
    slot = pl.program_id(0) % 2
    for w_ref, w16 in pairs:
        rows = w_ref.shape[0]
        r0 = pl.multiple_of(pl.program_id(1) * rows, rows)
        w16[slot, pl.ds(r0, rows), :] = w_ref[...].astype(BF16)


def _run_streamed(pairs, compute):
    jj = pl.program_id(0)

    @pl.when(jj == 0)
    def _():
        _stream_cast(pairs)

    @pl.when(jj > 0)
    def _():
        _stream_cast(pairs)
        compute(1 - jj % 2)


def _w16_scratch(k_rows, tn):
    return pltpu.VMEM((2, k_rows, tn), BF16)


def _ada_kernel(c_ref, w_ref, b_ref, o_ref):
    c = c_ref[...]
    c_act = (c * _sigmoid(c)).astype(BF16)
    o_ref[0] = _dot(c_act, w_ref[0].astype(BF16)) + b_ref[0]


def _ada(c, w_ada, b_ada):
    depth, d, n = w_ada.shape
    tn = min(512, n)
    c8 = jnp.broadcast_to(c, (SUBLANES, d))
    out = pl.pallas_call(
        _ada_kernel,
        grid=(depth, n // tn),
        in_specs=[pl.BlockSpec((SUBLANES, d), lambda l, j: (0, 0)),
                  pl.BlockSpec((1, d, tn), lambda l, j: (l, 0, j)),
                  pl.BlockSpec((1, 1, tn), lambda l, j: (l, 0, j))],
        out_specs=pl.BlockSpec((1, SUBLANES, tn), lambda l, j: (l, 0, j)),
        out_shape=jax.ShapeDtypeStruct((depth, SUBLANES, n), F32),
        compiler_params=_params(2),
        name="ada_mod",
    )(c8, w_ada, b_ada.reshape(depth, 1, n))
    return out[:, 0, :]


def _rms(x):
    return x * lax.rsqrt(jnp.mean(x * x, axis=-1, keepdims=True) + EPS)


def _norm_mod_kernel(x_ref, g_ref, sc_ref, sh_ref, o_ref):
    y = _rms(x_ref[...]) * g_ref[...]
    o_ref[...] = (y * (1.0 + sc_ref[...]) + sh_ref[...]).astype(o_ref.dtype)


def _norm_mod(x, g, scale, shift):
    t, d = x.shape
    tm = min(512, t)
    row = pl.BlockSpec((1, d), lambda i: (0, 0))
    return pl.pallas_call(
        _norm_mod_kernel,
        grid=(t // tm,),
        in_specs=[pl.BlockSpec((tm, d), lambda i: (i, 0)), row, row, row],
        out_specs=pl.BlockSpec((tm, d), lambda i: (i, 0)),
        out_shape=jax.ShapeDtypeStruct((t, d), BF16),
        compiler_params=_params(1),
        name="norm_mod",
    )(x, g, scale, shift)


def _final_norm_kernel(x_ref, g_ref, o_ref):
    o_ref[...] = _rms(x_ref[...]) * g_ref[...]


def _final_norm(x, g):
    t, d = x.shape
    tm = min(512, t)
    return pl.pallas_call(
        _final_norm_kernel,
        grid=(t // tm,),
        in_specs=[pl.BlockSpec((tm, d), lambda i: (i, 0)), pl.BlockSpec((1, d), lambda i: (0, 0))],
        out_specs=pl.BlockSpec((tm, d), lambda i: (i, 0)),
        out_shape=jax.ShapeDtypeStruct((t, d), F32),
        compiler_params=_params(1),
        name="final_norm",
    )(x, g)


def _rope_table_kernel(pos_ref, f_ref, c_ref, sa_ref, sb_ref):
    ang = pos_ref[...] * f_ref[...]
    cos, sin = jnp.cos(ang), jnp.sin(ang)
    lane = lax.broadcasted_iota(jnp.int32, ang.shape, 1)
    c_ref[...] = jnp.where(lane < ROT_DIM, cos, 1.0)
    sa_ref[...] = jnp.where(lane < ROT_HALF, -sin, 0.0)
    sb_ref[...] = jnp.where((lane >= ROT_HALF) & (lane < ROT_DIM), sin, 0.0)


def _rope_tables(positions):
    t = positions.shape[-1]
    tm = min(1024, t)
    inv_freq = ROPE_THETA ** (-jnp.arange(0, ROT_DIM, 2, dtype=F32) / ROT_DIM)
    f_row = jnp.concatenate([inv_freq, inv_freq, jnp.zeros((LANES - ROT_DIM,), F32)])[None, :]
    pos = positions.astype(F32).reshape(t, 1)
    tab = jax.ShapeDtypeStruct((t, LANES), F32)
    blk = pl.BlockSpec((tm, LANES), lambda i: (i, 0))
    return pl.pallas_call(
        _rope_table_kernel,
        grid=(t // tm,),
        in_specs=[pl.BlockSpec((tm, 1), lambda i: (i, 0)), pl.BlockSpec((1, LANES), lambda i: (0, 0))],
        out_specs=[blk, blk, blk],
        out_shape=[tab, tab, tab],
        compiler_params=_params(1),
        name="rope_tables",
    )(pos, f_row)


def _proj_sconv_kernel(h_ref, wb_ref, wc_ref, wx_ref, cw_ref, o_ref, wb16, wc16, wx16, ext_ref):
    tm = h_ref.shape[0]

    def compute(slot):
        @pl.when(pl.program_id(1) == 0)
        def _():
            ext_ref[0:SUBLANES, :] = jnp.zeros((SUBLANES, ext_ref.shape[1]), F32)

        h = h_ref[...]
        p = _dot(h, wc16[slot]) * _dot(h, wx16[slot])
        ext_ref[SUBLANES:, :] = p
        conv = (cw_ref[0:1, :] * ext_ref[pl.ds(SUBLANES - 2, tm), :]
                + cw_ref[1:2, :] * ext_ref[pl.ds(SUBLANES - 1, tm), :]
                + cw_ref[2:3, :] * p)
        o_ref[...] = (_dot(h, wb16[slot]) * conv).astype(o_ref.dtype)
        ext_ref[0:SUBLANES, :] = ext_ref[tm:tm + SUBLANES, :]

    _run_streamed(((wb_ref, wb16), (wc_ref, wc16), (wx_ref, wx16)), compute)


def _proj_sconv(h, w_in, layer, conv_w, col0):
    t, d = h.shape
    tm, tn = min(1024, t), MXU_COLS
    nb = SC_WIDTH // tn
    n_inner = t // tm
    c0 = col0 // tn

    def wspec(sec):
        return _stream_weight_spec(layer, d, n_inner, tn, nb, lambda j: c0 + sec * nb + j)

    return pl.pallas_call(
        _proj_sconv_kernel,
        grid=(nb + 1, n_inner),
        in_specs=[_row_spec(tm, d), wspec(0), wspec(1), wspec(2), _col_spec(SC_KERNEL, tn)],
        out_specs=_tile_spec(tm, tn),
        out_shape=jax.ShapeDtypeStruct((t, SC_WIDTH), BF16),
        scratch_shapes=[_w16_scratch(d, tn)] * 3 + [pltpu.VMEM((tm + SUBLANES, tn), F32)],
        compiler_params=_params(2),
        name="proj_sconv",
    )(h, w_in, w_in, w_in, conv_w)


def _proj_glu_kernel(h_ref, wa_ref, wg_ref, o_ref, wa16, wg16):
    def compute(slot):
        h = h_ref[...]
        for c0 in range(0, o_ref.shape[1], MXU_COLS):
            cs = slice(c0, c0 + MXU_COLS)
            o_ref[:, cs] = _dot(h, wa16[slot, :, cs]) * _sigmoid(_dot(h, wg16[slot, :, cs]))

    _run_streamed(((wa_ref, wa16), (wg_ref, wg16)), compute)


def _proj_glu(h, w_in, layer, col0):
    t, d = h.shape
    tm, tn = min(1024, t), 512
    nb = CF_WIDTH // tn
    n_inner = t // tm
    c0 = col0 // tn
    return pl.pallas_call(
        _proj_glu_kernel,
        grid=(nb + 1, n_inner),
        in_specs=[_row_spec(tm, d),
                  _stream_weight_spec(layer, d, n_inner, tn, nb, lambda j: c0 + j),
                  _stream_weight_spec(layer, d, n_inner, tn, nb, lambda j: c0 + nb + j)],
        out_specs=_tile_spec(tm, tn),
        out_shape=jax.ShapeDtypeStruct((t, CF_WIDTH), F32),
        scratch_shapes=[_w16_scratch(d, tn)] * 2,
        compiler_params=_params(2),
        name="proj_glu",
    )(h, w_in, w_in)


def _proj_qkv_kernel(h_ref, w_ref, c_ref, sa_ref, sb_ref, o_ref, w16, slab_ref, *, dilation):
    tm = h_ref.shape[0]
    tn = w16.shape[2]
    half = MXU_COLS
    rows = tm // dilation

    def compute(slot):
        is_v = pl.program_id(0) == 3
        c = jnp.where(is_v, 1.0, c_ref[...])
        sa = jnp.where(is_v, 0.0, sa_ref[...])
        sb = jnp.where(is_v, 0.0, sb_ref[...])
        h = h_ref[...]
        for c0 in range(0, tn, half):
            acc = _dot(h, w16[slot, :, c0:c0 + half])
            for s in range(half // LANES):
                x = acc[:, s * LANES:(s + 1) * LANES]
                y = x * c + pltpu.roll(x, LANES - ROT_HALF, 1) * sa + pltpu.roll(x, ROT_HALF, 1) * sb
                lanes = slice(c0 + s * LANES, c0 + (s + 1) * LANES)
                if dilation == 1:
                    o_ref[0, :, lanes] = y.astype(o_ref.dtype)
                else:
                    slab = c0 // LANES + s
                    slab_ref[slab] = y
                    for r in range(dilation):
                        o_ref[r, :, lanes] = slab_ref[slab, pl.ds(r, rows, stride=dilation), :].astype(o_ref.dtype)

    _run_streamed(((w_ref, w16),), compute)


def _proj_qkv(h, w_in, layer, tables, col_q, group, dilation):
    t, d = h.shape
    tm, tn = min(1024, t), DW_GROUP_WIDTH
    n_inner = t // tm
    c0 = col_q // tn + group
    sec_stride = DW_WIDTH // tn
    tab = _row_spec(tm, LANES)
    return pl.pallas_call(
        functools.partial(_proj_qkv_kernel, dilation=dilation),
        grid=(3 + 1, n_inner),
        in_specs=[_row_spec(tm, d),
                  _stream_weight_spec(layer, d, n_inner, tn, 3, lambda j: c0 + j * sec_stride),
                  tab, tab, tab],
        out_specs=pl.BlockSpec((None, dilation, tm // dilation, tn),
                               lambda jj, i: (jnp.maximum(jj - 1, 0), 0, jnp.where(jj == 0, 0, i), 0)),
        out_shape=jax.ShapeDtypeStruct((3, dilation, t // dilation, tn), BF16),
        scratch_shapes=[_w16_scratch(d, tn), pltpu.VMEM((tn // LANES, tm, LANES), F32)],
        compiler_params=_params(2),
        name=f"proj_qkv_d{dilation}",
    )(h, w_in, *tables)


def _activate(acc, act):
    if act == "sigmoid":
        return _sigmoid(acc)
    if act == "relu2":
        return jnp.square(jnp.maximum(acc, 0.0))
    raise ValueError(act)


def _proj_act_kernel(h_ref, w_ref, *rest, act, cast_side):
    if cast_side:
        side_ref, o_ref, side16_ref, w16 = rest
    else:
        o_ref, w16 = rest
    tn = o_ref.shape[1]
    half = MXU_COLS

    def compute(slot):
        h = h_ref[...]
        for c0 in range(0, tn, half):
            cs = slice(c0, c0 + half)
            o_ref[:, cs] = _activate(_dot(h, w16[slot, :, cs]), act).astype(o_ref.dtype)
        if cast_side:
            side16_ref[...] = side_ref[...].astype(BF16)

    _run_streamed(((w_ref, w16),), compute)


def _proj_act(h, w, layer, col0, width, act, name, side=None):
    t, d = h.shape
    tm = min(1024, t)
    tn = 1024 if width % 1024 == 0 and col0 % 1024 == 0 else 512
    nb = width // tn
    n_inner = t // tm
    c0 = col0 // tn
    in_specs = [_row_spec(tm, d), _stream_weight_spec(layer, d, n_inner, tn, nb, lambda j: c0 + j)]
    out_specs = [_tile_spec(tm, tn)]
    out_shape = [jax.ShapeDtypeStruct((t, width), BF16)]
    args = [h, w]
    if side is not None:
        _, side_rows, side_cols = side.shape
        blk = side_rows // (nb * n_inner)
        assert blk * nb * n_inner == side_rows and blk % 16 == 0

        def side_block(jj, i):
            return jnp.where(jj == 0, 0, (jj - 1) * n_inner + i)

        in_specs.append(pl.BlockSpec((None, blk, side_cols), lambda jj, i: (layer, side_block(jj, i), 0)))
        out_specs.append(pl.BlockSpec((blk, side_cols), lambda jj, i: (side_block(jj, i), 0)))
        out_shape.append(jax.ShapeDtypeStruct((side_rows, side_cols), BF16))
        args.append(side)
    outs = pl.pallas_call(
        functools.partial(_proj_act_kernel, act=act, cast_side=side is not None),
        grid=(nb + 1, n_inner),
        in_specs=in_specs,
        out_specs=out_specs,
        out_shape=out_shape,
        scratch_shapes=[_w16_scratch(d, tn)],
        compiler_params=_params(2),
        name=name,
    )(*args)
    return outs if side is not None else outs[0]


CF_ROWS = 64
CF_COLS = 256


def _conformer_kernel(uh_ref, u_ref, w_ref, b_ref, g_ref, be_ref, o_ref, sh_ref, y_ref):
    tm, width = u_ref.shape
    n_ext = tm + CF_HALO
    sh_ref[0, 0:CF_HALO, :] = jnp.where(pl.program_id(0) > 0, uh_ref[...], 0.0)
    sh_ref[0, CF_HALO:, :] = u_ref[...]
    for s in range(1, SUBLANES):
        sh_ref[s, 0:n_ext - SUBLANES, :] = sh_ref[0, pl.ds(s, n_ext - SUBLANES), :]
    first = CF_HALO - (CF_KERNEL - 1)
    for r0 in range(0, tm, CF_ROWS):
        for c0 in range(0, width, CF_COLS):
            cs = slice(c0, c0 + CF_COLS)
            acc = jnp.broadcast_to(b_ref[:, cs], (CF_ROWS, CF_COLS))
            for k in range(CF_KERNEL):
                shift = (first + k) % SUBLANES
                base = r0 + first + k - shift
                acc = acc + w_ref[k:k + 1, cs] * sh_ref[shift, base:base + CF_ROWS, cs]
            y_ref[r0:r0 + CF_ROWS, cs] = acc
    y = y_ref[...]
    mu = jnp.mean(y, axis=-1, keepdims=True)
    yc = y - mu
    var = jnp.mean(yc * yc, axis=-1, keepdims=True)
    z = yc * lax.rsqrt(var + EPS) * g_ref[...] + be_ref[...]
    o_ref[...] = (z * _sigmoid(z)).astype(o_ref.dtype)


def _conformer(u, conv_w, bias, ln_g, ln_b):
    t, width = u.shape
    tm = min(256, t)
    per = tm // CF_HALO
    row = pl.BlockSpec((1, width), lambda i: (0, 0))
    return pl.pallas_call(
        _conformer_kernel,
        grid=(t // tm,),
        in_specs=[pl.BlockSpec((CF_HALO, width), lambda i: (jnp.maximum(i * per - 1, 0), 0)),
                  pl.BlockSpec((tm, width), lambda i: (i, 0)),
                  pl.BlockSpec((CF_KERNEL, width), lambda i: (0, 0)), row, row, row],
        out_specs=pl.BlockSpec((tm, width), lambda i: (i, 0)),
        out_shape=jax.ShapeDtypeStruct((t, width), BF16),
        scratch_shapes=[pltpu.VMEM((SUBLANES, tm + CF_HALO, width), F32), pltpu.VMEM((tm, width), F32)],
        compiler_params=_params(1),
        name="conformer_conv",
    )(u, u, conv_w, bias, ln_g, ln_b)


def _attn_kernel(q_ref, kp_ref, kc_ref, vp_ref, vc_ref, o_ref, l_ref):
    n = ATTN_BLOCK
    n_blocks = q_ref.shape[0] // n
    k = jnp.concatenate([kp_ref[...], kc_ref[...]], axis=0)
    v = jnp.concatenate([vp_ref[...], vc_ref[...]], axis=0)
    qi = lax.broadcasted_iota(jnp.int32, (n, 2 * n), 0)
    kj = lax.broadcasted_iota(jnp.int32, (n, 2 * n), 1)
    band = (kj >= qi) & (kj <= qi + n)
    has_prev = pl.program_id(1) > 0
    l_ref[...] = jnp.zeros(l_ref.shape, F32)
    for jb in range(n_blocks):
        rows = slice(jb * n, (jb + 1) * n)
        keys = slice(jb * n, (jb + 2) * n)
        valid = band & ((kj >= n) | has_prev) if jb == 0 else band
        for h in range(DW_HEADS):
            sl = slice(h * HEAD_DIM, (h + 1) * HEAD_DIM)
            s = lax.dot_general(q_ref[rows, sl], k[keys, sl], (((1,), (1,)), ((), ())),
                                preferred_element_type=F32) * (HEAD_DIM ** -0.5)
            s = jnp.where(valid, s, -jnp.inf)
            m = jnp.max(s, axis=-1, keepdims=True)
            p = jnp.exp(s - m)
            den = jnp.sum(p, axis=-1, keepdims=True)
            o_ref[rows, sl] = _dot((p / den).astype(BF16), v[keys, sl])
            l_ref[rows, h:h + 1] = m + jnp.log(den)


def _attention_group(qkv):
    _, dilation, sub, gw = qkv.shape
    n = ATTN_BLOCK
    assert sub % n == 0
    per_step = min(4, sub // n)
    tq = per_step * n
    assert sub % tq == 0

    def cur(sec):
        return pl.BlockSpec((None, None, tq, gw), lambda r, b: (sec, r, b, 0))

    def prev(sec):
        return pl.BlockSpec((None, None, n, gw), lambda r, b: (sec, r, jnp.maximum(b * per_step - 1, 0), 0))

    return pl.pallas_call(
        _attn_kernel,
        grid=(dilation, sub // tq),
        in_specs=[cur(0), prev(1), cur(1), prev(2), cur(2)],
        out_specs=[pl.BlockSpec((None, tq, gw), lambda r, b: (r, b, 0)),
                   pl.BlockSpec((None, tq, LANES), lambda r, b: (r, b, 0))],
        out_shape=[jax.ShapeDtypeStruct((dilation, sub, gw), F32),
                   jax.ShapeDtypeStruct((dilation, sub, LANES), F32)],
        compiler_params=_params(2),
        name=f"dilated_attn_d{dilation}",
    )(qkv, qkv, qkv, qkv, qkv)


def _combine_kernel(o0_ref, o1_ref, o2_ref, l0_ref, l1_ref, l2_ref, out_ref, lnat_ref, onat_ref):
    o_refs = (o0_ref, o1_ref, o2_ref)
    l_refs = (l0_ref, l1_ref, l2_ref)
    tm = out_ref.shape[0]

    def natural(dst_ref, src, dilation):
        rows = tm // dilation
        for r in range(dilation):
            dst_ref[pl.ds(r, rows, stride=dilation), :] = src(r)

    lse = []
    for g, l_ref in enumerate(l_refs):
        dilation = l_ref.shape[0]
        if dilation == 1:
            lse.append(l_ref[0])
        else:
            natural(lnat_ref.at[g], lambda r, l_ref=l_ref: l_ref[r], dilation)
            lse.append(lnat_ref[g])
    m = jnp.maximum(jnp.maximum(lse[0], lse[1]), lse[2])
    e = [jnp.exp(l - m) for l in lse]
    den = e[0] + e[1] + e[2]
    w = [x / den for x in e]
    for h in range(DW_HEADS):
        sl = slice(h * HEAD_DIM, (h + 1) * HEAD_DIM)
        acc = None
        for g, o_ref in enumerate(o_refs):
            dilation = o_ref.shape[0]
            if dilation == 1:
                o_nat = o_ref[0, :, sl]
            else:
                natural(onat_ref, lambda r, o_ref=o_ref: o_ref[r, :, sl], dilation)
                o_nat = onat_ref[...]
            term = w[g][:, h:h + 1] * o_nat
            acc = term if acc is None else acc + term
        out_ref[:, sl] = acc.astype(out_ref.dtype)


def _combine(outs, lses):
    gw = outs[0].shape[-1]
    t = outs[0].shape[0] * outs[0].shape[1]
    tm = min(512, t)

    def spec(a):
        dil = a.shape[0]
        return pl.BlockSpec((dil, tm // dil, a.shape[-1]), lambda i: (0, i, 0))

    return pl.pallas_call(
        _combine_kernel,
        grid=(t // tm,),
        in_specs=[spec(a) for a in outs] + [spec(a) for a in lses],
        out_specs=pl.BlockSpec((tm, gw), lambda i: (i, 0)),
        out_shape=jax.ShapeDtypeStruct((t, gw), BF16),
        scratch_shapes=[pltpu.VMEM((N_DW_GROUPS, tm, LANES), F32), pltpu.VMEM((tm, LANES), F32)],
        compiler_params=_params(1),
        name="attn_combine",
    )(*outs, *lses)


def _merge_kernel(za_ref, zb_ref, zc_ref, wa_ref, wb_ref, wc_ref, ga_ref, gb_ref, gc_ref, o_ref,
                  wa16, wb16, wc16):
    def compute(slot):
        m = ga_ref[...].astype(F32) * _dot(za_ref[...], wa16[slot])
        m = m + gb_ref[...].astype(F32) * _dot(zb_ref[...], wb16[slot])
        m = m + gc_ref[...].astype(F32) * _dot(zc_ref[...], wc16[slot])
        o_ref[...] = m.astype(o_ref.dtype)

    _run_streamed(((wa_ref, wa16), (wb_ref, wb16), (wc_ref, wc16)), compute)


def _merge(za, zb, zc, wa, wb, wc, layer, gates):
    t = za.shape[0]
    d = wa.shape[-1]
    tm, tn = min(1024, t), min(512, d)
    nb = d // tn
    n_inner = t // tm

    def wgt(a):
        return _stream_weight_spec(layer, a.shape[1], n_inner, tn, nb, lambda j: j)

    def gate(sec):
        return _tile_spec(tm, tn, lambda j: sec * nb + j)

    return pl.pallas_call(
        _merge_kernel,
        grid=(nb + 1, n_inner),
        in_specs=[_row_spec(tm, za.shape[1]), _row_spec(tm, zb.shape[1]), _row_spec(tm, zc.shape[1]),
                  wgt(wa), wgt(wb), wgt(wc), gate(0), gate(1), gate(2)],
        out_specs=_tile_spec(tm, tn),
        out_shape=jax.ShapeDtypeStruct((t, d), BF16),
        scratch_shapes=[_w16_scratch(a.shape[1], tn) for a in (wa, wb, wc)],
        compiler_params=_params(2),
        name="merge_branches",
    )(za, zb, zc, wa, wb, wc, gates, gates, gates)


def _out_proj_kernel(a_ref, w_ref, x_ref, g_ref, o_ref, w16):
    def compute(slot):
        o_ref[...] = x_ref[...] + g_ref[...] * _dot(a_ref[...], w16[slot])

    _run_streamed(((w_ref, w16),), compute)


def _out_proj(a, w, layer, x, gate):
    t, kdim = a.shape
    d = w.shape[-1]
    tm, tn = min(1024, t), min(512, d)
    nb = d // tn
    n_inner = t // tm
    return pl.pallas_call(
        _out_proj_kernel,
        grid=(nb + 1, n_inner),
        in_specs=[_row_spec(tm, kdim), _stream_weight_spec(layer, kdim, n_inner, tn, nb, lambda j: j),
                  _tile_spec(tm, tn), _col_spec(1, tn)],
        out_specs=_tile_spec(tm, tn),
        out_shape=jax.ShapeDtypeStruct((t, d), F32),
        scratch_shapes=[_w16_scratch(kdim, tn)],
        compiler_params=_params(2),
        name="out_proj_resid",
    )(a, w, x, gate)


def _resid_kernel(a_ref, w_ref, x_ref, g_ref, o_ref, acc_ref):
    kk = pl.program_id(2)

    @pl.when(kk == 0)
    def _():
        acc_ref[...] = jnp.zeros(acc_ref.shape, F32)

    acc_ref[...] += _dot(a_ref[...], w_ref[...])

    @pl.when(kk == pl.num_programs(2) - 1)
    def _():
        o_ref[...] = x_ref[...] + g_ref[...] * acc_ref[...]


def _resid_matmul(a, w, x, gate, name):
    t, kdim = a.shape
    d = w.shape[1]
    tm, tn, tk = min(1024, t), min(1024, d), min(2048, kdim)
    return pl.pallas_call(
        _resid_kernel,
        grid=(d // tn, t // tm, kdim // tk),
        in_specs=[pl.BlockSpec((tm, tk), lambda j, i, k: (i, k)),
                  pl.BlockSpec((tk, tn), lambda j, i, k: (k, j)),
                  pl.BlockSpec((tm, tn), lambda j, i, k: (i, j)),
                  pl.BlockSpec((1, tn), lambda j, i, k: (0, j))],
        out_specs=pl.BlockSpec((tm, tn), lambda j, i, k: (i, j)),
        out_shape=jax.ShapeDtypeStruct((t, d), F32),
        scratch_shapes=[pltpu.VMEM((tm, tn), F32)],
        compiler_params=_params(3),
        name=name,
    )(a, w, x, gate)


def kernel(x, c, positions, w_ada, b_ada, g_mix, w_in, conv_a, conv_b, conv_b_bias, ln_cf_g, ln_cf_b,
           w_out_a, w_out_b, w_out_c, w_o, g_mlp, w_mlp1, w_mlp2, g_final):
    b, t, d = x.shape
    assert b == 1, "kernel written for a single sequence"
    depth = w_ada.shape[0]
    d_ff = w_mlp1.shape[-1]
    xs = x[0]

    mod = _ada(c, w_ada, b_ada).reshape(depth, N_MOD, 1, d)
    tables = _rope_tables(positions[0])

    col_sc = 0
    col_cf = col_sc + 3 * SC_WIDTH
    col_q = col_cf + 2 * CF_WIDTH
    col_gate = col_q + 3 * DW_WIDTH

    for l in range(depth):
        shift1, scale1, gate1, shift2, scale2, gate2 = (mod[l, i] for i in range(N_MOD))

        h = _norm_mod(xs, g_mix[l][None, :], scale1, shift1)
        z_a = _proj_sconv(h, w_in, l, conv_a[l], col_sc)
        u = _proj_glu(h, w_in, l, col_cf)
        z_b = _conformer(u, conv_b[l], conv_b_bias[l][None, :], ln_cf_g[l][None, :], ln_cf_b[l][None, :])

        outs, lses = [], []
        for g, (window, dilation) in enumerate(DW_PATTERNS):
            assert window // dilation == ATTN_BLOCK
            o_g, lse_g = _attention_group(_proj_qkv(h, w_in, l, tables, col_q, g, dilation))
            outs.append(o_g)
            lses.append(lse_g)
        z_c = _combine(outs, lses)

        gates = _proj_act(h, w_in, l, col_gate, 3 * d, "sigmoid", "proj_gates")
        merged = _merge(z_a, z_b, z_c, w_out_a, w_out_b, w_out_c, l, gates)
        xs = _out_proj(merged, w_o, l, xs, gate1)

        h2 = _norm_mod(xs, g_mlp[l][None, :], scale2, shift2)
        a, w_down = _proj_act(h2, w_mlp1, l, 0, d_ff, "relu2", "mlp_up", side=w_mlp2)
        xs = _resid_matmul(a, w_down, xs, gate2, "mlp_down_resid")

    return _final_norm(xs, g_final[None, :])[None]
```

```python
import functools

import jax
import jax.numpy as jnp
from jax import lax
from jax.experimental import pallas as pl
from jax.experimental.pallas import tpu as pltpu

F32 = jnp.float32
BF16 = jnp.bfloat16

EPS = 1e-6
SC_WIDTH = 2048
SC_KERNEL = 3
CF_WIDTH = 2048
CF_KERNEL = 31
HEAD_DIM = 128
DW_PATTERNS = ((128, 1), (512, 4), (2048, 16))
N_DW_GROUPS = len(DW_PATTERNS)
DW_HEADS = 8
DW_GROUP_WIDTH = DW_HEADS * HEAD_DIM
DW_WIDTH = N_DW_GROUPS * DW_GROUP_WIDTH
ATTN_BLOCK = 128
ROT_DIM = HEAD_DIM // 4
ROT_HALF = ROT_DIM // 2
ROPE_THETA = 500000.0
N_MOD = 6

LANES = 128
SUBLANES = 8
MXU_COLS = 256
CF_HALO = 32
VMEM_LIMIT = 58 * 1024 * 1024


def _params(n_axes):
    return pltpu.CompilerParams(dimension_semantics=("arbitrary",) * n_axes,
                                vmem_limit_bytes=VMEM_LIMIT)


def _dot(a, b):
    return jnp.dot(a, b, preferred_element_type=F32)


def _sigmoid(v):
    return jax.nn.sigmoid(v)


def _stream_weight_spec(layer, k_rows, n_chunks, tn, n_col_tiles, col_block):
    chunk = k_rows // n_chunks
    assert chunk * n_chunks == k_rows and chunk % 16 == 0

    def imap(jj, i):
        return (layer, jnp.where(jj < n_col_tiles, i, 0), col_block(jnp.minimum(jj, n_col_tiles - 1)))

    return pl.BlockSpec((None, chunk, tn), imap)


def _row_spec(tm, width):
    return pl.BlockSpec((tm, width), lambda jj, i: (jnp.where(jj == 0, 0, i), 0))


def _tile_spec(tm, tn, col_block=lambda j: j):
    return pl.BlockSpec((tm, tn), lambda jj, i: (jnp.where(jj == 0, 0, i), col_block(jnp.maximum(jj - 1, 0))))


def _col_spec(rows, tn):
    return pl.BlockSpec((rows, tn), lambda jj, i: (0, jnp.maximum(jj - 1, 0)))


def _stream_cast(pairs):
    for w_ref, w16 in pairs:
        rows = w_ref.shape[0]
        r0 = pl.multiple_of(pl.program_id(1) * rows, rows)
        w16[pl.ds(r0, rows), :] = w_ref[...].astype(BF16)


def _run_streamed(streams, compute):
    jj = pl.program_id(0)
    fill_a = [(w, a) for w, a, _ in streams]
    fill_b = [(w, b) for w, _, b in streams]

    @pl.when(jj == 0)
    def _():
        _stream_cast(fill_a)

    @pl.when(jj % 2 == 1)
    def _():
        _stream_cast(fill_b)
        compute([a for _, a, _ in streams])

    @pl.when((jj > 0) & (jj % 2 == 0))
    def _():
        _stream_cast(fill_a)
        compute([b for _, _, b in streams])


def _w16_scratch(k_rows, tn):
    return [pltpu.VMEM((k_rows, tn), BF16)] * 2


def _ada_kernel(c_ref, w_ref, b_ref, o_ref):
    c = c_ref[...]
    c_act = (c * _sigmoid(c)).astype(BF16)
    o_ref[0] = _dot(c_act, w_ref[0].astype(BF16)) + b_ref[0]


def _ada(c, w_ada, b_ada):
    depth, d, n = w_ada.shape
    tn = min(512, n)
    c8 = jnp.broadcast_to(c, (SUBLANES, d))
    out = pl.pallas_call(
        _ada_kernel,
        grid=(depth, n // tn),
        in_specs=[pl.BlockSpec((SUBLANES, d), lambda l, j: (0, 0)),
                  pl.BlockSpec((1, d, tn), lambda l, j: (l, 0, j)),
                  pl.BlockSpec((1, 1, tn), lambda l, j: (l, 0, j))],
        out_specs=pl.BlockSpec((1, SUBLANES, tn), lambda l, j: (l, 0, j)),
        out_shape=jax.ShapeDtypeStruct((depth, SUBLANES, n), F32),
        compiler_params=_params(2),
        name="ada_mod",
    )(c8, w_ada, b_ada.reshape(depth, 1, n))
    return out[:, 0, :]


def _rms(x):
    return x * lax.rsqrt(jnp.mean(x * x, axis=-1, keepdims=True) + EPS)


def _norm_mod_kernel(x_ref, g_ref, sc_ref, sh_ref, o_ref):
    y = _rms(x_ref[...]) * g_ref[...]
    o_ref[...] = (y * (1.0 + sc_ref[...]) + sh_ref[...]).astype(o_ref.dtype)


def _norm_mod(x, g, scale, shift):
    t, d = x.shape
    tm = min(512, t)
    row = pl.BlockSpec((1, d), lambda i: (0, 0))
    return pl.pallas_call(
        _norm_mod_kernel,
        grid=(t // tm,),
        in_specs=[pl.BlockSpec((tm, d), lambda i: (i, 0)), row, row, row],
        out_specs=pl.BlockSpec((tm, d), lambda i: (i, 0)),
        out_shape=jax.ShapeDtypeStruct((t, d), BF16),
        compiler_params=_params(1),
        name="norm_mod",
    )(x, g, scale, shift)


def _final_norm_kernel(x_ref, g_ref, o_ref):
    o_ref[...] = _rms(x_ref[...]) * g_ref[...]


def _final_norm(x, g):
    t, d = x.shape
    tm = min(512, t)
    return pl.pallas_call(
        _final_norm_kernel,
        grid=(t // tm,),
        in_specs=[pl.BlockSpec((tm, d), lambda i: (i, 0)), pl.BlockSpec((1, d), lambda i: (0, 0))],
        out_specs=pl.BlockSpec((tm, d), lambda i: (i, 0)),
        out_shape=jax.ShapeDtypeStruct((t, d), F32),
        compiler_params=_params(1),
        name="final_norm",
    )(x, g)


def _rope_table_kernel(pos_ref, f_ref, c_ref, sa_ref, sb_ref):
    ang = pos_ref[...] * f_ref[...]
    cos, sin = jnp.cos(ang), jnp.sin(ang)
    lane = lax.broadcasted_iota(jnp.int32, ang.shape, 1)
    c_ref[...] = jnp.where(lane < ROT_DIM, cos, 1.0)
    sa_ref[...] = jnp.where(lane < ROT_HALF, -sin, 0.0)
    sb_ref[...] = jnp.where((lane >= ROT_HALF) & (lane < ROT_DIM), sin, 0.0)


def _rope_tables(positions):
    t = positions.shape[-1]
    tm = min(1024, t)
    inv_freq = ROPE_THETA ** (-jnp.arange(0, ROT_DIM, 2, dtype=F32) / ROT_DIM)
    f_row = jnp.concatenate([inv_freq, inv_freq, jnp.zeros((LANES - ROT_DIM,), F32)])[None, :]
    pos = positions.astype(F32).reshape(t, 1)
    tab = jax.ShapeDtypeStruct((t, LANES), F32)
    blk = pl.BlockSpec((tm, LANES), lambda i: (i, 0))
    return pl.pallas_call(
        _rope_table_kernel,
        grid=(t // tm,),
        in_specs=[pl.BlockSpec((tm, 1), lambda i: (i, 0)), pl.BlockSpec((1, LANES), lambda i: (0, 0))],
        out_specs=[blk, blk, blk],
        out_shape=[tab, tab, tab],
        compiler_params=_params(1),
        name="rope_tables",
    )(pos, f_row)


def _proj_sconv_kernel(h_ref, wb_ref, wc_ref, wx_ref, cw_ref, o_ref, wb_a, wb_b, wc_a, wc_b, wx_a, wx_b, ext_ref):
    tm = h_ref.shape[0]

    def compute(w16):
        wb16, wc16, wx16 = w16

        @pl.when(pl.program_id(1) == 0)
        def _():
            ext_ref[0:SUBLANES, :] = jnp.zeros((SUBLANES, ext_ref.shape[1]), F32)

        h = h_ref[...]
        p = _dot(h, wc16[...]) * _dot(h, wx16[...])
        ext_ref[SUBLANES:, :] = p
        conv = (cw_ref[0:1, :] * ext_ref[pl.ds(SUBLANES - 2, tm), :]
                + cw_ref[1:2, :] * ext_ref[pl.ds(SUBLANES - 1, tm), :]
                + cw_ref[2:3, :] * p)
        o_ref[...] = (_dot(h, wb16[...]) * conv).astype(o_ref.dtype)
        ext_ref[0:SUBLANES, :] = ext_ref[tm:tm + SUBLANES, :]

    _run_streamed(((wb_ref, wb_a, wb_b), (wc_ref, wc_a, wc_b), (wx_ref, wx_a, wx_b)), compute)


def _proj_sconv(h, w_in, layer, conv_w, col0):
    t, d = h.shape
    tm, tn = min(1024, t), MXU_COLS
    nb = SC_WIDTH // tn
    n_inner = t // tm
    c0 = col0 // tn

    def wspec(sec):
        return _stream_weight_spec(layer, d, n_inner, tn, nb, lambda j: c0 + sec * nb + j)

    return pl.pallas_call(
        _proj_sconv_kernel,
        grid=(nb + 1, n_inner),
        in_specs=[_row_spec(tm, d), wspec(0), wspec(1), wspec(2), _col_spec(SC_KERNEL, tn)],
        out_specs=_tile_spec(tm, tn),
        out_shape=jax.ShapeDtypeStruct((t, SC_WIDTH), BF16),
        scratch_shapes=_w16_scratch(d, tn) * 3 + [pltpu.VMEM((tm + SUBLANES, tn), F32)],
        compiler_params=_params(2),
        name="proj_sconv",
    )(h, w_in, w_in, w_in, conv_w)


def _proj_glu_kernel(h_ref, wa_ref, wg_ref, o_ref, wa_a, wa_b, wg_a, wg_b):
    def compute(w16):
        wa16, wg16 = w16
        h = h_ref[...]
        for c0 in range(0, o_ref.shape[1], MXU_COLS):
            cs = slice(c0, c0 + MXU_COLS)
            o_ref[:, cs] = _dot(h, wa16[:, cs]) * _sigmoid(_dot(h, wg16[:, cs]))

    _run_streamed(((wa_ref, wa_a, wa_b), (wg_ref, wg_a, wg_b)), compute)


def _proj_glu(h, w_in, layer, col0):
    t, d = h.shape
    tm, tn = min(1024, t), 512
    nb = CF_WIDTH // tn
    n_inner = t // tm
    c0 = col0 // tn
    return pl.pallas_call(
        _proj_glu_kernel,
        grid=(nb + 1, n_inner),
        in_specs=[_row_spec(tm, d),
                  _stream_weight_spec(layer, d, n_inner, tn, nb, lambda j: c0 + j),
                  _stream_weight_spec(layer, d, n_inner, tn, nb, lambda j: c0 + nb + j)],
        out_specs=_tile_spec(tm, tn),
        out_shape=jax.ShapeDtypeStruct((t, CF_WIDTH), F32),
        scratch_shapes=_w16_scratch(d, tn) * 2,
        compiler_params=_params(2),
        name="proj_glu",
    )(h, w_in, w_in)


def _proj_qkv_kernel(h_ref, w_ref, c_ref, sa_ref, sb_ref, o_ref, w_a, w_b, slab_ref, *, dilation):
    tm = h_ref.shape[0]
    tn = w_a.shape[1]
    half = MXU_COLS
    rows = tm // dilation

    def compute(w16, rope):
        c, sa, sb = c_ref[...], sa_ref[...], sb_ref[...]
        h = h_ref[...]
        for c0 in range(0, tn, half):
            acc = _dot(h, w16[:, c0:c0 + half])
            for s in range(half // LANES):
                y = acc[:, s * LANES:(s + 1) * LANES]
                if rope:
                    y = y * c + pltpu.roll(y, LANES - ROT_HALF, 1) * sa + pltpu.roll(y, ROT_HALF, 1) * sb
                lanes = slice(c0 + s * LANES, c0 + (s + 1) * LANES)
                if dilation == 1:
                    o_ref[0, :, lanes] = y.astype(o_ref.dtype)
                else:
                    slab = c0 // LANES + s
                    slab_ref[slab] = y
                    for r in range(dilation):
                        o_ref[r, :, lanes] = slab_ref[slab, pl.ds(r, rows, stride=dilation), :].astype(o_ref.dtype)

    jj = pl.program_id(0)
    for step, (use, fill, rope) in enumerate(((None, w_a, False), (w_a, w_b, True), (w_b, w_a, True),
                                              (w_a, w_b, False))):
        @pl.when(jj == step)
        def _(use=use, fill=fill, rope=rope):
            _stream_cast([(w_ref, fill)])
            if use is not None:
                compute(use, rope)


def _proj_qkv(h, w_in, layer, tables, col_q, group, dilation):
    t, d = h.shape
    tm, tn = min(1024, t), DW_GROUP_WIDTH
    n_inner = t // tm
    c0 = col_q // tn + group
    sec_stride = DW_WIDTH // tn
    tab = _row_spec(tm, LANES)
    return pl.pallas_call(
        functools.partial(_proj_qkv_kernel, dilation=dilation),
        grid=(3 + 1, n_inner),
        in_specs=[_row_spec(tm, d),
                  _stream_weight_spec(layer, d, n_inner, tn, 3, lambda j: c0 + j * sec_stride),
                  tab, tab, tab],
        out_specs=pl.BlockSpec((None, dilation, tm // dilation, tn),
                               lambda jj, i: (jnp.maximum(jj - 1, 0), 0, jnp.where(jj == 0, 0, i), 0)),
        out_shape=jax.ShapeDtypeStruct((3, dilation, t // dilation, tn), BF16),
        scratch_shapes=_w16_scratch(d, tn) + [pltpu.VMEM((tn // LANES, tm, LANES), F32)],
        compiler_params=_params(2),
        name=f"proj_qkv_d{dilation}",
    )(h, w_in, *tables)


def _activate(acc, act):
    if act == "sigmoid":
        return _sigmoid(acc)
    if act == "relu2":
        return jnp.square(jnp.maximum(acc, 0.0))
    raise ValueError(act)


def _proj_act_kernel(h_ref, w_ref, *rest, act, cast_side):
    if cast_side:
        side_ref, o_ref, side16_ref, w_a, w_b = rest
    else:
        o_ref, w_a, w_b = rest
    tn = o_ref.shape[1]
    half = MXU_COLS

    def compute(w16):
        h = h_ref[...]
        for c0 in range(0, tn, half):
            cs = slice(c0, c0 + half)
            o_ref[:, cs] = _activate(_dot(h, w16[0][:, cs]), act).astype(o_ref.dtype)
        if cast_side:
            side16_ref[...] = side_ref[...].astype(BF16)

    _run_streamed(((w_ref, w_a, w_b),), compute)


def _proj_act(h, w, layer, col0, width, act, name, side=None):
    t, d = h.shape
    tm = min(1024, t)
    tn = 1024 if width % 1024 == 0 and col0 % 1024 == 0 else 512
    nb = width // tn
    n_inner = t // tm
    c0 = col0 // tn
    in_specs = [_row_spec(tm, d), _stream_weight_spec(layer, d, n_inner, tn, nb, lambda j: c0 + j)]
    out_specs = [_tile_spec(tm, tn)]
    out_shape = [jax.ShapeDtypeStruct((t, width), BF16)]
    args = [h, w]
    if side is not None:
        _, side_rows, side_cols = side.shape
        blk = side_rows // (nb * n_inner)
        assert blk * nb * n_inner == side_rows and blk % 16 == 0

        def side_block(jj, i):
            return jnp.where(jj == 0, 0, (jj - 1) * n_inner + i)

        in_specs.append(pl.BlockSpec((None, blk, side_cols), lambda jj, i: (layer, side_block(jj, i), 0)))
        out_specs.append(pl.BlockSpec((blk, side_cols), lambda jj, i: (side_block(jj, i), 0)))
        out_shape.append(jax.ShapeDtypeStruct((side_rows, side_cols), BF16))
        args.append(side)
    outs = pl.pallas_call(
        functools.partial(_proj_act_kernel, act=act, cast_side=side is not None),
        grid=(nb + 1, n_inner),
        in_specs=in_specs,
        out_specs=out_specs,
        out_shape=out_shape,
        scratch_shapes=_w16_scratch(d, tn),
        compiler_params=_params(2),
        name=name,
    )(*args)
    return outs if side is not None else outs[0]


CF_ROWS = 64
CF_COLS = 256


def _conformer_kernel(uh_ref, u_ref, w_ref, b_ref, g_ref, be_ref, o_ref, sh_ref, y_ref):
    tm, width = u_ref.shape
    n_ext = tm + CF_HALO
    sh_ref[0, 0:CF_HALO, :] = jnp.where(pl.program_id(0) > 0, uh_ref[...], 0.0)
    sh_ref[0, CF_HALO:, :] = u_ref[...]
    for s in range(1, SUBLANES):
        sh_ref[s, 0:n_ext - SUBLANES, :] = sh_ref[0, pl.ds(s, n_ext - SUBLANES), :]
    first = CF_HALO - (CF_KERNEL - 1)

    def row_chunk(rc, carry):
        r0 = pl.multiple_of(rc * CF_ROWS, CF_ROWS)
        for c0 in range(0, width, CF_COLS):
            cs = slice(c0, c0 + CF_COLS)
            acc = jnp.broadcast_to(b_ref[:, cs], (CF_ROWS, CF_COLS))
            for k in range(CF_KERNEL):
                shift = (first + k) % SUBLANES
                base = pl.multiple_of(r0 + (first + k - shift), SUBLANES)
                acc = acc + w_ref[k:k + 1, cs] * sh_ref[shift, pl.ds(base, CF_ROWS), cs]
            y_ref[pl.ds(r0, CF_ROWS), cs] = acc
        return carry

    lax.fori_loop(0, tm // CF_ROWS, row_chunk, 0)
    y = y_ref[...]
    mu = jnp.mean(y, axis=-1, keepdims=True)
    yc = y - mu
    var = jnp.mean(yc * yc, axis=-1, keepdims=True)
    z = yc * lax.rsqrt(var + EPS) * g_ref[...] + be_ref[...]
    o_ref[...] = (z * _sigmoid(z)).astype(o_ref.dtype)


def _conformer(u, conv_w, bias, ln_g, ln_b):
    t, width = u.shape
    tm = min(256, t)
    per = tm // CF_HALO
    row = pl.BlockSpec((1, width), lambda i: (0, 0))
    return pl.pallas_call(
        _conformer_kernel,
        grid=(t // tm,),
        in_specs=[pl.BlockSpec((CF_HALO, width), lambda i: (jnp.maximum(i * per - 1, 0), 0)),
                  pl.BlockSpec((tm, width), lambda i: (i, 0)),
                  pl.BlockSpec((CF_KERNEL, width), lambda i: (0, 0)), row, row, row],
        out_specs=pl.BlockSpec((tm, width), lambda i: (i, 0)),
        out_shape=jax.ShapeDtypeStruct((t, width), BF16),
        scratch_shapes=[pltpu.VMEM((SUBLANES, tm + CF_HALO, width), F32), pltpu.VMEM((tm, width), F32)],
        compiler_params=_params(1),
        name="conformer_conv",
    )(u, u, conv_w, bias, ln_g, ln_b)


def _attn_kernel(q_ref, kp_ref, kc_ref, vp_ref, vc_ref, o_ref, l_ref):
    n = ATTN_BLOCK
    n_blocks = q_ref.shape[0] // n
    k = jnp.concatenate([kp_ref[...], kc_ref[...]], axis=0)
    v = jnp.concatenate([vp_ref[...], vc_ref[...]], axis=0)
    qi = lax.broadcasted_iota(jnp.int32, (n, 2 * n), 0)
    kj = lax.broadcasted_iota(jnp.int32, (n, 2 * n), 1)
    band = (kj >= qi) & (kj <= qi + n)
    has_prev = pl.program_id(1) > 0
    l_ref[...] = jnp.zeros(l_ref.shape, F32)
    for jb in range(n_blocks):
        rows = slice(jb * n, (jb + 1) * n)
        keys = slice(jb * n, (jb + 2) * n)
        valid = band & ((kj >= n) | has_prev) if jb == 0 else band
        for h in range(DW_HEADS):
            sl = slice(h * HEAD_DIM, (h + 1) * HEAD_DIM)
            s = lax.dot_general(q_ref[rows, sl], k[keys, sl], (((1,), (1,)), ((), ())),
                                preferred_element_type=F32) * (HEAD_DIM ** -0.5)
            s = jnp.where(valid, s, -jnp.inf)
            m = jnp.max(s, axis=-1, keepdims=True)
            p = jnp.exp(s - m)
            den = jnp.sum(p, axis=-1, keepdims=True)
            o_ref[rows, sl] = _dot((p / den).astype(BF16), v[keys, sl])
            l_ref[rows, h:h + 1] = m + jnp.log(den)


def _attention_group(qkv):
    _, dilation, sub, gw = qkv.shape
    n = ATTN_BLOCK
    assert sub % n == 0
    per_step = min(4, sub // n)
    tq = per_step * n
    assert sub % tq == 0

    def cur(sec):
        return pl.BlockSpec((None, None, tq, gw), lambda r, b: (sec, r, b, 0))

    def prev(sec):
        return pl.BlockSpec((None, None, n, gw), lambda r, b: (sec, r, jnp.maximum(b * per_step - 1, 0), 0))

    return pl.pallas_call(
        _attn_kernel,
        grid=(dilation, sub // tq),
        in_specs=[cur(0), prev(1), cur(1), prev(2), cur(2)],
        out_specs=[pl.BlockSpec((None, tq, gw), lambda r, b: (r, b, 0)),
                   pl.BlockSpec((None, tq, LANES), lambda r, b: (r, b, 0))],
        out_shape=[jax.ShapeDtypeStruct((dilation, sub, gw), F32),
                   jax.ShapeDtypeStruct((dilation, sub, LANES), F32)],
        compiler_params=_params(2),
        name=f"dilated_attn_d{dilation}",
    )(qkv, qkv, qkv, qkv, qkv)


def _combine_kernel(o0_ref, o1_ref, o2_ref, l0_ref, l1_ref, l2_ref, out_ref, lnat_ref, onat_ref):
    o_refs = (o0_ref, o1_ref, o2_ref)
    l_refs = (l0_ref, l1_ref, l2_ref)
    tm = out_ref.shape[0]

    def natural(dst_ref, src, dilation):
        rows = tm // dilation
        for r in range(dilation):
            dst_ref[pl.ds(r, rows, stride=dilation), :] = src(r)

    lse = []
    for g, l_ref in enumerate(l_refs):
        dilation = l_ref.shape[0]
        if dilation == 1:
            lse.append(l_ref[0])
        else:
            natural(lnat_ref.at[g], lambda r, l_ref=l_ref: l_ref[r], dilation)
            lse.append(lnat_ref[g])
    m = jnp.maximum(jnp.maximum(lse[0], lse[1]), lse[2])
    e = [jnp.exp(l - m) for l in lse]
    den = e[0] + e[1] + e[2]
    w = [x / den for x in e]
    for h in range(DW_HEADS):
        sl = slice(h * HEAD_DIM, (h + 1) * HEAD_DIM)
        acc = None
        for g, o_ref in enumerate(o_refs):
            dilation = o_ref.shape[0]
            if dilation == 1:
                o_nat = o_ref[0, :, sl]
            else:
                natural(onat_ref, lambda r, o_ref=o_ref: o_ref[r, :, sl], dilation)
                o_nat = onat_ref[...]
            term = w[g][:, h:h + 1] * o_nat
            acc = term if acc is None else acc + term
        out_ref[:, sl] = acc.astype(out_ref.dtype)


def _combine(outs, lses):
    gw = outs[0].shape[-1]
    t = outs[0].shape[0] * outs[0].shape[1]
    tm = min(512, t)

    def spec(a):
        dil = a.shape[0]
        return pl.BlockSpec((dil, tm // dil, a.shape[-1]), lambda i: (0, i, 0))

    return pl.pallas_call(
        _combine_kernel,
        grid=(t // tm,),
        in_specs=[spec(a) for a in outs] + [spec(a) for a in lses],
        out_specs=pl.BlockSpec((tm, gw), lambda i: (i, 0)),
        out_shape=jax.ShapeDtypeStruct((t, gw), BF16),
        scratch_shapes=[pltpu.VMEM((N_DW_GROUPS, tm, LANES), F32), pltpu.VMEM((tm, LANES), F32)],
        compiler_params=_params(1),
        name="attn_combine",
    )(*outs, *lses)


def _merge_kernel(za_ref, zb_ref, zc_ref, wa_ref, wb_ref, wc_ref, ga_ref, gb_ref, gc_ref, o_ref,
                  wa_a, wa_b, wb_a, wb_b, wc_a, wc_b):
    def compute(w16):
        wa16, wb16, wc16 = w16
        m = ga_ref[...].astype(F32) * _dot(za_ref[...], wa16[...])
        m = m + gb_ref[...].astype(F32) * _dot(zb_ref[...], wb16[...])
        m = m + gc_ref[...].astype(F32) * _dot(zc_ref[...], wc16[...])
        o_ref[...] = m.astype(o_ref.dtype)

    _run_streamed(((wa_ref, wa_a, wa_b), (wb_ref, wb_a, wb_b), (wc_ref, wc_a, wc_b)), compute)


def _merge(za, zb, zc, wa, wb, wc, layer, gates):
    t = za.shape[0]
    d = wa.shape[-1]
    tm, tn = min(1024, t), min(512, d)
    nb = d // tn
    n_inner = t // tm

    def wgt(a):
        return _stream_weight_spec(layer, a.shape[1], n_inner, tn, nb, lambda j: j)

    def gate(sec):
        return _tile_spec(tm, tn, lambda j: sec * nb + j)

    return pl.pallas_call(
        _merge_kernel,
        grid=(nb + 1, n_inner),
        in_specs=[_row_spec(tm, za.shape[1]), _row_spec(tm, zb.shape[1]), _row_spec(tm, zc.shape[1]),
                  wgt(wa), wgt(wb), wgt(wc), gate(0), gate(1), gate(2)],
        out_specs=_tile_spec(tm, tn),
        out_shape=jax.ShapeDtypeStruct((t, d), BF16),
        scratch_shapes=[buf for a in (wa, wb, wc) for buf in _w16_scratch(a.shape[1], tn)],
        compiler_params=_params(2),
        name="merge_branches",
    )(za, zb, zc, wa, wb, wc, gates, gates, gates)


def _out_proj_kernel(a_ref, w_ref, x_ref, g_ref, o_ref, w_a, w_b):
    def compute(w16):
        o_ref[...] = x_ref[...] + g_ref[...] * _dot(a_ref[...], w16[0][...])

    _run_streamed(((w_ref, w_a, w_b),), compute)


def _out_proj(a, w, layer, x, gate):
    t, kdim = a.shape
    d = w.shape[-1]
    tm, tn = min(1024, t), min(512, d)
    nb = d // tn
    n_inner = t // tm
    return pl.pallas_call(
        _out_proj_kernel,
        grid=(nb + 1, n_inner),
        in_specs=[_row_spec(tm, kdim), _stream_weight_spec(layer, kdim, n_inner, tn, nb, lambda j: j),
                  _tile_spec(tm, tn), _col_spec(1, tn)],
        out_specs=_tile_spec(tm, tn),
        out_shape=jax.ShapeDtypeStruct((t, d), F32),
        scratch_shapes=_w16_scratch(kdim, tn),
        compiler_params=_params(2),
        name="out_proj_resid",
    )(a, w, x, gate)


def _resid_kernel(a_ref, w_ref, x_ref, g_ref, o_ref, acc_ref):
    kk = pl.program_id(2)
    last = pl.num_programs(2) - 1

    @pl.when(kk == 0)
    def _():
        acc_ref[...] = _dot(a_ref[...], w_ref[...])

    @pl.when((kk > 0) & (kk < last))
    def _():
        acc_ref[...] += _dot(a_ref[...], w_ref[...])

    @pl.when(kk == last)
    def _():
        o_ref[...] = x_ref[...] + g_ref[...] * (acc_ref[...] + _dot(a_ref[...], w_ref[...]))


def _resid_matmul(a, w, x, gate, name):
    t, kdim = a.shape
    d = w.shape[1]
    tm, tn, tk = min(1024, t), min(1024, d), min(2048, kdim // 2)
    return pl.pallas_call(
        _resid_kernel,
        grid=(d // tn, t // tm, kdim // tk),
        in_specs=[pl.BlockSpec((tm, tk), lambda j, i, k: (i, k)),
                  pl.BlockSpec((tk, tn), lambda j, i, k: (k, j)),
                  pl.BlockSpec((tm, tn), lambda j, i, k: (i, j)),
                  pl.BlockSpec((1, tn), lambda j, i, k: (0, j))],
        out_specs=pl.BlockSpec((tm, tn), lambda j, i, k: (i, j)),
        out_shape=jax.ShapeDtypeStruct((t, d), F32),
        scratch_shapes=[pltpu.VMEM((tm, tn), F32)],
        compiler_params=_params(3),
        name=name,
    )(a, w, x, gate)


def kernel(x, c, positions, w_ada, b_ada, g_mix, w_in, conv_a, conv_b, conv_b_bias, ln_cf_g, ln_cf_b,
           w_out_a, w_out_b, w_out_c, w_o, g_mlp, w_mlp1, w_mlp2, g_final):
    b, t, d = x.shape
    assert b == 1, "kernel written for a single sequence"
    depth = w_ada.shape[0]
    d_ff = w_mlp1.shape[-1]
    xs = x[0]

    mod = _ada(c, w_ada, b_ada).reshape(depth, N_MOD, 1, d)
    tables = _rope_tables(positions[0])

    col_sc = 0
    col_cf = col_sc + 3 * SC_WIDTH
    col_q = col_cf + 2 * CF_WIDTH
    col_gate = col_q + 3 * DW_WIDTH

    for l in range(depth):
        shift1, scale1, gate1, shift2, scale2, gate2 = (mod[l, i] for i in range(N_MOD))

        h = _norm_mod(xs, g_mix[l][None, :], scale1, shift1)
        z_a = _proj_sconv(h, w_in, l, conv_a[l], col_sc)
        u = _proj_glu(h, w_in, l, col_cf)
        z_b = _conformer(u, conv_b[l], conv_b_bias[l][None, :], ln_cf_g[l][None, :], ln_cf_b[l][None, :])

        outs, lses = [], []
        for g, (window, dilation) in enumerate(DW_PATTERNS):
            assert window // dilation == ATTN_BLOCK
            o_g, lse_g = _attention_group(_proj_qkv(h, w_in, l, tables, col_q, g, dilation))
            outs.append(o_g)
            lses.append(lse_g)
        z_c = _combine(outs, lses)

        gates = _proj_act(h, w_in, l, col_gate, 3 * d, "sigmoid", "proj_gates")
        merged = _merge(z_a, z_b, z_c, w_out_a, w_out_b, w_out_c, l, gates)
        xs = _out_proj(merged, w_o, l, xs, gate1)

        h2 = _norm_mod(xs, g_mlp[l][None, :], scale2, shift2)
        a, w_down = _proj_act(h2, w_mlp1, l, 0, d_ff, "relu2", "mlp_up", side=w_mlp2)
        xs = _resid_matmul(a, w_down, xs, gate2, "mlp_down_resid")

    return _final_norm(xs, g_final[None, :])[None]
```

```python
import functools

import jax
import jax.numpy as jnp
from jax import lax
from jax.experimental import pallas as pl
from jax.experimental.pallas import tpu as pltpu

F32 = jnp.float32
BF16 = jnp.bfloat16

EPS = 1e-6
SC_WIDTH = 2048
SC_KERNEL = 3
CF_WIDTH = 2048
CF_KERNEL = 31
HEAD_DIM = 128
DW_PATTERNS = ((128, 1), (512, 4), (2048, 16))
N_DW_GROUPS = len(DW_PATTERNS)
DW_HEADS = 8
DW_GROUP_WIDTH = DW_HEADS * HEAD_DIM
DW_WIDTH = N_DW_GROUPS * DW_GROUP_WIDTH
ATTN_BLOCK = 128
ROT_DIM = HEAD_DIM // 4
ROT_HALF = ROT_DIM // 2
ROPE_THETA = 500000.0
N_MOD = 6

LANES = 128
SUBLANES = 8
MXU_COLS = 256
CF_HALO = 32
VMEM_LIMIT = 61 * 1024 * 1024


def _params(n_axes):
    return pltpu.CompilerParams(dimension_semantics=("arbitrary",) * n_axes,
                                vmem_limit_bytes=VMEM_LIMIT)


def _dot(a, b):
    return jnp.dot(a, b, preferred_element_type=F32)


def _sigmoid(v):
    return jax.nn.sigmoid(v)


def _stream_weight_spec(layer, k_rows, n_chunks, tn, n_col_tiles, col_block):
    chunk = k_rows // n_chunks
    assert chunk * n_chunks == k_rows and chunk % 16 == 0

    def imap(jj, i):
        return (layer, jnp.where(jj < n_col_tiles, i, 0), col_block(jnp.minimum(jj, n_col_tiles - 1)))

    return pl.BlockSpec((None, chunk, tn), imap)


def _row_spec(tm, width):
    return pl.BlockSpec((tm, width), lambda jj, i: (jnp.where(jj == 0, 0, i), 0))


def _tile_spec(tm, tn, col_block=lambda j: j):
    return pl.BlockSpec((tm, tn), lambda jj, i: (jnp.where(jj == 0, 0, i), col_block(jnp.maximum(jj - 1, 0))))


def _col_spec(rows, tn):
    return pl.BlockSpec((rows, tn), lambda jj, i: (0, jnp.maximum(jj - 1, 0)))


def _stream_cast(pairs):
    for w_ref, w16 in pairs:
        rows = w_ref.shape[0]
        r0 = pl.multiple_of(pl.program_id(1) * rows, rows)
        w16[pl.ds(r0, rows), :] = w_ref[...].astype(BF16)


def _run_streamed(streams, compute):
    jj = pl.program_id(0)
    fill_a = [(w, a) for w, a, _ in streams]
    fill_b = [(w, b) for w, _, b in streams]

    @pl.when(jj == 0)
    def _():
        _stream_cast(fill_a)

    @pl.when(jj % 2 == 1)
    def _():
        _stream_cast(fill_b)
        compute([a for _, a, _ in streams])

    @pl.when((jj > 0) & (jj % 2 == 0))
    def _():
        _stream_cast(fill_a)
        compute([b for _, _, b in streams])


def _w16_scratch(k_rows, tn):
    return [pltpu.VMEM((k_rows, tn), BF16)] * 2


def _ada_block(cb_ref, w_ref, b_ref):
    cb = cb_ref[...]
    c_act = cb * _sigmoid(cb)
    parts = [jnp.sum(w_ref[:, j:j + LANES] * c_act, axis=0, keepdims=True)
             for j in range(0, w_ref.shape[1], LANES)]
    return jnp.concatenate(parts, axis=1) + b_ref[...]


def _ada_kernel(cb_ref, w_ref, b_ref, o_ref):
    o_ref[...] = _ada_block(cb_ref, w_ref, b_ref)


def _lane_broadcast(c):
    return jnp.broadcast_to(c.reshape(-1, 1), (c.size, LANES))


def _ada(c, w_ada, b_ada, layer):
    _, d, n = w_ada.shape
    tn = min(512, n)
    return pl.pallas_call(
        _ada_kernel,
        grid=(n // tn,),
        in_specs=[pl.BlockSpec((d, LANES), lambda j: (0, 0)),
                  pl.BlockSpec((None, d, tn), lambda j: (layer, 0, j)),
                  pl.BlockSpec((None, 1, tn), lambda j: (layer, 0, j))],
        out_specs=pl.BlockSpec((1, tn), lambda j: (0, j)),
        out_shape=jax.ShapeDtypeStruct((1, n), F32),
        compiler_params=_params(1),
        name="ada_mod",
    )(_lane_broadcast(c), w_ada, b_ada.reshape(-1, 1, n))


def _rms(x):
    return x * lax.rsqrt(jnp.mean(x * x, axis=-1, keepdims=True) + EPS)


def _norm_mod_kernel(x_ref, g_ref, sc_ref, sh_ref, o_ref):
    y = _rms(x_ref[...]) * g_ref[...]
    o_ref[...] = (y * (1.0 + sc_ref[...]) + sh_ref[...]).astype(o_ref.dtype)


def _norm_mod(x, g, scale, shift):
    t, d = x.shape
    tm = min(512, t)
    row = pl.BlockSpec((1, d), lambda i: (0, 0))
    return pl.pallas_call(
        _norm_mod_kernel,
        grid=(t // tm,),
        in_specs=[pl.BlockSpec((tm, d), lambda i: (i, 0)), row, row, row],
        out_specs=pl.BlockSpec((tm, d), lambda i: (i, 0)),
        out_shape=jax.ShapeDtypeStruct((t, d), BF16),
        compiler_params=_params(1),
        name="norm_mod",
    )(x, g, scale, shift)


def _final_norm_kernel(x_ref, g_ref, o_ref):
    o_ref[...] = _rms(x_ref[...]) * g_ref[...]


def _final_norm(x, g):
    t, d = x.shape
    tm = min(512, t)
    return pl.pallas_call(
        _final_norm_kernel,
        grid=(t // tm,),
        in_specs=[pl.BlockSpec((tm, d), lambda i: (i, 0)), pl.BlockSpec((1, d), lambda i: (0, 0))],
        out_specs=pl.BlockSpec((tm, d), lambda i: (i, 0)),
        out_shape=jax.ShapeDtypeStruct((t, d), F32),
        compiler_params=_params(1),
        name="final_norm",
    )(x, g)


def _rope_table_kernel(pos_ref, f_ref, c_ref, sa_ref, sb_ref):
    ang = pos_ref[...] * f_ref[...]
    cos, sin = jnp.cos(ang), jnp.sin(ang)
    lane = lax.broadcasted_iota(jnp.int32, ang.shape, 1)
    c_ref[...] = jnp.where(lane < ROT_DIM, cos, 1.0)
    sa_ref[...] = jnp.where(lane < ROT_HALF, -sin, 0.0)
    sb_ref[...] = jnp.where((lane >= ROT_HALF) & (lane < ROT_DIM), sin, 0.0)


def _rope_tables(positions):
    t = positions.shape[-1]
    tm = min(1024, t)
    inv_freq = ROPE_THETA ** (-jnp.arange(0, ROT_DIM, 2, dtype=F32) / ROT_DIM)
    f_row = jnp.concatenate([inv_freq, inv_freq, jnp.zeros((LANES - ROT_DIM,), F32)])[None, :]
    pos = positions.astype(F32).reshape(t, 1)
    tab = jax.ShapeDtypeStruct((t, LANES), F32)
    blk = pl.BlockSpec((tm, LANES), lambda i: (i, 0))
    return pl.pallas_call(
        _rope_table_kernel,
        grid=(t // tm,),
        in_specs=[pl.BlockSpec((tm, 1), lambda i: (i, 0)), pl.BlockSpec((1, LANES), lambda i: (0, 0))],
        out_specs=[blk, blk, blk],
        out_shape=[tab, tab, tab],
        compiler_params=_params(1),
        name="rope_tables",
    )(pos, f_row)


def _proj_sconv_kernel(h_ref, wb_ref, wc_ref, wx_ref, cw_ref, o_ref, wb_a, wb_b, wc_a, wc_b, wx_a, wx_b, ext_ref):
    tm = h_ref.shape[0]

    def compute(w16):
        wb16, wc16, wx16 = w16

        @pl.when(pl.program_id(1) == 0)
        def _():
            ext_ref[0:SUBLANES, :] = jnp.zeros((SUBLANES, ext_ref.shape[1]), F32)

        h = h_ref[...]
        p = _dot(h, wc16[...]) * _dot(h, wx16[...])
        ext_ref[SUBLANES:, :] = p
        conv = (cw_ref[0:1, :] * ext_ref[pl.ds(SUBLANES - 2, tm), :]
                + cw_ref[1:2, :] * ext_ref[pl.ds(SUBLANES - 1, tm), :]
                + cw_ref[2:3, :] * p)
        o_ref[...] = (_dot(h, wb16[...]) * conv).astype(o_ref.dtype)
        ext_ref[0:SUBLANES, :] = ext_ref[tm:tm + SUBLANES, :]

    _run_streamed(((wb_ref, wb_a, wb_b), (wc_ref, wc_a, wc_b), (wx_ref, wx_a, wx_b)), compute)


def _proj_sconv(h, w_in, layer, conv_w, col0):
    t, d = h.shape
    tm, tn = min(1024, t), MXU_COLS
    nb = SC_WIDTH // tn
    n_inner = t // tm
    c0 = col0 // tn

    def wspec(sec):
        return _stream_weight_spec(layer, d, n_inner, tn, nb, lambda j: c0 + sec * nb + j)

    return pl.pallas_call(
        _proj_sconv_kernel,
        grid=(nb + 1, n_inner),
        in_specs=[_row_spec(tm, d), wspec(0), wspec(1), wspec(2), _col_spec(SC_KERNEL, tn)],
        out_specs=_tile_spec(tm, tn),
        out_shape=jax.ShapeDtypeStruct((t, SC_WIDTH), BF16),
        scratch_shapes=_w16_scratch(d, tn) * 3 + [pltpu.VMEM((tm + SUBLANES, tn), F32)],
        compiler_params=_params(2),
        name="proj_sconv",
    )(h, w_in, w_in, w_in, conv_w)


def _proj_glu_kernel(h_ref, wa_ref, wg_ref, o_ref, wa_a, wa_b, wg_a, wg_b):
    def compute(w16):
        wa16, wg16 = w16
        h = h_ref[...]
        for c0 in range(0, o_ref.shape[1], MXU_COLS):
            cs = slice(c0, c0 + MXU_COLS)
            o_ref[:, cs] = _dot(h, wa16[:, cs]) * _sigmoid(_dot(h, wg16[:, cs]))

    _run_streamed(((wa_ref, wa_a, wa_b), (wg_ref, wg_a, wg_b)), compute)


def _proj_glu(h, w_in, layer, col0):
    t, d = h.shape
    tm, tn = min(1024, t), 512
    nb = CF_WIDTH // tn
    n_inner = t // tm
    c0 = col0 // tn
    return pl.pallas_call(
        _proj_glu_kernel,
        grid=(nb + 1, n_inner),
        in_specs=[_row_spec(tm, d),
                  _stream_weight_spec(layer, d, n_inner, tn, nb, lambda j: c0 + j),
                  _stream_weight_spec(layer, d, n_inner, tn, nb, lambda j: c0 + nb + j)],
        out_specs=_tile_spec(tm, tn),
        out_shape=jax.ShapeDtypeStruct((t, CF_WIDTH), F32),
        scratch_shapes=_w16_scratch(d, tn) * 2,
        compiler_params=_params(2),
        name="proj_glu",
    )(h, w_in, w_in)


def _proj_qkv_kernel(h_ref, w_ref, c_ref, sa_ref, sb_ref, o_ref, w_a, w_b, slab_ref, *, dilation):
    tm = h_ref.shape[0]
    tn = w_a.shape[1]
    half = MXU_COLS
    rows = tm // dilation

    def compute(w16, rope):
        c, sa, sb = c_ref[...], sa_ref[...], sb_ref[...]
        h = h_ref[...]
        for c0 in range(0, tn, half):
            acc = _dot(h, w16[:, c0:c0 + half])
            for s in range(half // LANES):
                y = acc[:, s * LANES:(s + 1) * LANES]
                if rope:
                    y = y * c + pltpu.roll(y, LANES - ROT_HALF, 1) * sa + pltpu.roll(y, ROT_HALF, 1) * sb
                lanes = slice(c0 + s * LANES, c0 + (s + 1) * LANES)
                if dilation == 1:
                    o_ref[0, :, lanes] = y.astype(o_ref.dtype)
                else:
                    slab = c0 // LANES + s
                    slab_ref[slab] = y
                    for r in range(dilation):
                        o_ref[r, :, lanes] = slab_ref[slab, pl.ds(r, rows, stride=dilation), :].astype(o_ref.dtype)

    jj = pl.program_id(0)
    for step, (use, fill, rope) in enumerate(((None, w_a, False), (w_a, w_b, True), (w_b, w_a, True),
                                              (w_a, w_b, False))):
        @pl.when(jj == step)
        def _(use=use, fill=fill, rope=rope):
            _stream_cast([(w_ref, fill)])
            if use is not None:
                compute(use, rope)


def _proj_qkv(h, w_in, layer, tables, col_q, group, dilation):
    t, d = h.shape
    tm, tn = min(1024, t), DW_GROUP_WIDTH
    n_inner = t // tm
    c0 = col_q // tn + group
    sec_stride = DW_WIDTH // tn
    tab = _row_spec(tm, LANES)
    return pl.pallas_call(
        functools.partial(_proj_qkv_kernel, dilation=dilation),
        grid=(3 + 1, n_inner),
        in_specs=[_row_spec(tm, d),
                  _stream_weight_spec(layer, d, n_inner, tn, 3, lambda j: c0 + j * sec_stride),
                  tab, tab, tab],
        out_specs=pl.BlockSpec((None, dilation, tm // dilation, tn),
                               lambda jj, i: (jnp.maximum(jj - 1, 0), 0, jnp.where(jj == 0, 0, i), 0)),
        out_shape=jax.ShapeDtypeStruct((3, dilation, t // dilation, tn), BF16),
        scratch_shapes=_w16_scratch(d, tn) + [pltpu.VMEM((tn // LANES, tm, LANES), F32)],
        compiler_params=_params(2),
        name=f"proj_qkv_d{dilation}",
    )(h, w_in, *tables)


def _activate(acc, act):
    if act == "sigmoid":
        return _sigmoid(acc)
    if act == "relu2":
        return jnp.square(jnp.maximum(acc, 0.0))
    raise ValueError(act)


def _proj_act_kernel(h_ref, w_ref, *rest, act, cast_side):
    if cast_side:
        side_ref, o_ref, side16_ref, w_a, w_b = rest
    else:
        o_ref, w_a, w_b = rest
    tn = o_ref.shape[1]
    half = MXU_COLS

    def compute(w16):
        h = h_ref[...]
        for c0 in range(0, tn, half):
            cs = slice(c0, c0 + half)
            o_ref[:, cs] = _activate(_dot(h, w16[0][:, cs]), act).astype(o_ref.dtype)
        if cast_side:
            side16_ref[...] = side_ref[...].astype(BF16)

    _run_streamed(((w_ref, w_a, w_b),), compute)


def _proj_act(h, w, layer, col0, width, act, name, side=None):
    t, d = h.shape
    tm = min(1024, t)
    tn = 1024 if width % 1024 == 0 and col0 % 1024 == 0 else 512
    nb = width // tn
    n_inner = t // tm
    c0 = col0 // tn
    in_specs = [_row_spec(tm, d), _stream_weight_spec(layer, d, n_inner, tn, nb, lambda j: c0 + j)]
    out_specs = [_tile_spec(tm, tn)]
    out_shape = [jax.ShapeDtypeStruct((t, width), BF16)]
    args = [h, w]
    if side is not None:
        _, side_rows, side_cols = side.shape
        blk = side_rows // (nb * n_inner)
        assert blk * nb * n_inner == side_rows and blk % 16 == 0

        def side_block(jj, i):
            return jnp.where(jj == 0, 0, (jj - 1) * n_inner + i)

        in_specs.append(pl.BlockSpec((None, blk, side_cols), lambda jj, i: (layer, side_block(jj, i), 0)))
        out_specs.append(pl.BlockSpec((blk, side_cols), lambda jj, i: (side_block(jj, i), 0)))
        out_shape.append(jax.ShapeDtypeStruct((side_rows, side_cols), BF16))
        args.append(side)
    outs = pl.pallas_call(
        functools.partial(_proj_act_kernel, act=act, cast_side=side is not None),
        grid=(nb + 1, n_inner),
        in_specs=in_specs,
        out_specs=out_specs,
        out_shape=out_shape,
        scratch_shapes=_w16_scratch(d, tn),
        compiler_params=_params(2),
        name=name,
    )(*args)
    return outs if side is not None else outs[0]


CF_ROWS = 64
CF_COLS = 256


def _conformer_tile(is_first, uh_ref, u_ref, w_ref, b_ref, g_ref, be_ref, o_ref, sh_ref, y_ref):
    tm, width = u_ref.shape
    n_ext = tm + CF_HALO
    sh_ref[0, 0:CF_HALO, :] = jnp.where(is_first, 0.0, uh_ref[...])
    sh_ref[0, CF_HALO:, :] = u_ref[...]
    for s in range(1, SUBLANES):
        sh_ref[s, 0:n_ext - SUBLANES, :] = sh_ref[0, pl.ds(s, n_ext - SUBLANES), :]
    first = CF_HALO - (CF_KERNEL - 1)

    def row_chunk(rc, carry):
        r0 = pl.multiple_of(rc * CF_ROWS, CF_ROWS)
        for c0 in range(0, width, CF_COLS):
            cs = slice(c0, c0 + CF_COLS)
            acc = jnp.broadcast_to(b_ref[:, cs], (CF_ROWS, CF_COLS))
            for k in range(CF_KERNEL):
                shift = (first + k) % SUBLANES
                base = pl.multiple_of(r0 + (first + k - shift), SUBLANES)
                acc = acc + w_ref[k:k + 1, cs] * sh_ref[shift, pl.ds(base, CF_ROWS), cs]
            y_ref[pl.ds(r0, CF_ROWS), cs] = acc
        return carry

    lax.fori_loop(0, tm // CF_ROWS, row_chunk, 0)
    y = y_ref[...]
    mu = jnp.mean(y, axis=-1, keepdims=True)
    yc = y - mu
    var = jnp.mean(yc * yc, axis=-1, keepdims=True)
    z = yc * lax.rsqrt(var + EPS) * g_ref[...] + be_ref[...]
    o_ref[...] = (z * _sigmoid(z)).astype(o_ref.dtype)


def _conformer_kernel(uh_ref, u_ref, w_ref, b_ref, g_ref, be_ref, o_ref, sh_ref, y_ref):
    _conformer_tile(pl.program_id(0) == 0, uh_ref, u_ref, w_ref, b_ref, g_ref, be_ref, o_ref, sh_ref, y_ref)


def _conformer(u, conv_w, bias, ln_g, ln_b):
    t, width = u.shape
    tm = min(256, t)
    per = tm // CF_HALO
    row = pl.BlockSpec((1, width), lambda i: (0, 0))
    return pl.pallas_call(
        _conformer_kernel,
        grid=(t // tm,),
        in_specs=[pl.BlockSpec((CF_HALO, width), lambda i: (jnp.maximum(i * per - 1, 0), 0)),
                  pl.BlockSpec((tm, width), lambda i: (i, 0)),
                  pl.BlockSpec((CF_KERNEL, width), lambda i: (0, 0)), row, row, row],
        out_specs=pl.BlockSpec((tm, width), lambda i: (i, 0)),
        out_shape=jax.ShapeDtypeStruct((t, width), BF16),
        scratch_shapes=[pltpu.VMEM((SUBLANES, tm + CF_HALO, width), F32), pltpu.VMEM((tm, width), F32)],
        compiler_params=_params(1),
        name="conformer_conv",
    )(u, u, conv_w, bias, ln_g, ln_b)


def _attn_kernel(q_ref, kp_ref, kc_ref, vp_ref, vc_ref, o_ref, l_ref):
    n = ATTN_BLOCK
    n_blocks = q_ref.shape[0] // n
    k = jnp.concatenate([kp_ref[...], kc_ref[...]], axis=0)
    v = jnp.concatenate([vp_ref[...], vc_ref[...]], axis=0)
    qi = lax.broadcasted_iota(jnp.int32, (n, 2 * n), 0)
    kj = lax.broadcasted_iota(jnp.int32, (n, 2 * n), 1)
    band = (kj >= qi) & (kj <= qi + n)
    has_prev = pl.program_id(1) > 0
    l_ref[...] = jnp.zeros(l_ref.shape, F32)
    for jb in range(n_blocks):
        rows = slice(jb * n, (jb + 1) * n)
        keys = slice(jb * n, (jb + 2) * n)
        valid = band & ((kj >= n) | has_prev) if jb == 0 else band
        for h in range(DW_HEADS):
            sl = slice(h * HEAD_DIM, (h + 1) * HEAD_DIM)
            s = lax.dot_general(q_ref[rows, sl], k[keys, sl], (((1,), (1,)), ((), ())),
                                preferred_element_type=F32) * (HEAD_DIM ** -0.5)
            s = jnp.where(valid, s, -jnp.inf)
            m = jnp.max(s, axis=-1, keepdims=True)
            p = jnp.exp(s - m)
            den = jnp.sum(p, axis=-1, keepdims=True)
            o_ref[rows, sl] = _dot((p / den).astype(BF16), v[keys, sl])
            l_ref[rows, h:h + 1] = m + jnp.log(den)


def _attention_group(qkv):
    _, dilation, sub, gw = qkv.shape
    n = ATTN_BLOCK
    assert sub % n == 0
    per_step = min(4, sub // n)
    tq = per_step * n
    assert sub % tq == 0

    def cur(sec):
        return pl.BlockSpec((None, None, tq, gw), lambda r, b: (sec, r, b, 0))

    def prev(sec):
        return pl.BlockSpec((None, None, n, gw), lambda r, b: (sec, r, jnp.maximum(b * per_step - 1, 0), 0))

    return pl.pallas_call(
        _attn_kernel,
        grid=(dilation, sub // tq),
        in_specs=[cur(0), prev(1), cur(1), prev(2), cur(2)],
        out_specs=[pl.BlockSpec((None, tq, gw), lambda r, b: (r, b, 0)),
                   pl.BlockSpec((None, tq, LANES), lambda r, b: (r, b, 0))],
        out_shape=[jax.ShapeDtypeStruct((dilation, sub, gw), F32),
                   jax.ShapeDtypeStruct((dilation, sub, LANES), F32)],
        compiler_params=_params(2),
        name=f"dilated_attn_d{dilation}",
    )(qkv, qkv, qkv, qkv, qkv)


def _combine_kernel(o0_ref, o1_ref, o2_ref, l0_ref, l1_ref, l2_ref, out_ref, lnat_ref, onat_ref):
    o_refs = (o0_ref, o1_ref, o2_ref)
    l_refs = (l0_ref, l1_ref, l2_ref)
    tm = out_ref.shape[0]

    def natural(dst_ref, src, dilation):
        rows = tm // dilation
        for r in range(dilation):
            dst_ref[pl.ds(r, rows, stride=dilation), :] = src(r)

    lse = []
    for g, l_ref in enumerate(l_refs):
        dilation = l_ref.shape[0]
        if dilation == 1:
            lse.append(l_ref[0])
        else:
            natural(lnat_ref.at[g], lambda r, l_ref=l_ref: l_ref[r], dilation)
            lse.append(lnat_ref[g])
    m = jnp.maximum(jnp.maximum(lse[0], lse[1]), lse[2])
    e = [jnp.exp(l - m) for l in lse]
    den = e[0] + e[1] + e[2]
    w = [x / den for x in e]
    for h in range(DW_HEADS):
        sl = slice(h * HEAD_DIM, (h + 1) * HEAD_DIM)
        acc = None
        for g, o_ref in enumerate(o_refs):
            dilation = o_ref.shape[0]
            if dilation == 1:
                o_nat = o_ref[0, :, sl]
            else:
                natural(onat_ref, lambda r, o_ref=o_ref: o_ref[r, :, sl], dilation)
                o_nat = onat_ref[...]
            term = w[g][:, h:h + 1] * o_nat
            acc = term if acc is None else acc + term
        out_ref[:, sl] = acc.astype(out_ref.dtype)


def _combine(outs, lses):
    gw = outs[0].shape[-1]
    t = outs[0].shape[0] * outs[0].shape[1]
    tm = min(512, t)

    def spec(a):
        dil = a.shape[0]
        return pl.BlockSpec((dil, tm // dil, a.shape[-1]), lambda i: (0, i, 0))

    return pl.pallas_call(
        _combine_kernel,
        grid=(t // tm,),
        in_specs=[spec(a) for a in outs] + [spec(a) for a in lses],
        out_specs=pl.BlockSpec((tm, gw), lambda i: (i, 0)),
        out_shape=jax.ShapeDtypeStruct((t, gw), BF16),
        scratch_shapes=[pltpu.VMEM((N_DW_GROUPS, tm, LANES), F32), pltpu.VMEM((tm, LANES), F32)],
        compiler_params=_params(1),
        name="attn_combine",
    )(*outs, *lses)


def _merge_kernel(za_ref, zb_ref, zc_ref, wa_ref, wb_ref, wc_ref, ga_ref, gb_ref, gc_ref, o_ref,
                  wa_a, wa_b, wb_a, wb_b, wc_a, wc_b):
    def compute(w16):
        wa16, wb16, wc16 = w16
        m = ga_ref[...].astype(F32) * _dot(za_ref[...], wa16[...])
        m = m + gb_ref[...].astype(F32) * _dot(zb_ref[...], wb16[...])
        m = m + gc_ref[...].astype(F32) * _dot(zc_ref[...], wc16[...])
        o_ref[...] = m.astype(o_ref.dtype)

    _run_streamed(((wa_ref, wa_a, wa_b), (wb_ref, wb_a, wb_b), (wc_ref, wc_a, wc_b)), compute)


def _merge(za, zb, zc, wa, wb, wc, layer, gates):
    t = za.shape[0]
    d = wa.shape[-1]
    tm, tn = min(1024, t), min(512, d)
    nb = d // tn
    n_inner = t // tm

    def wgt(a):
        return _stream_weight_spec(layer, a.shape[1], n_inner, tn, nb, lambda j: j)

    def gate(sec):
        return _tile_spec(tm, tn, lambda j: sec * nb + j)

    return pl.pallas_call(
        _merge_kernel,
        grid=(nb + 1, n_inner),
        in_specs=[_row_spec(tm, za.shape[1]), _row_spec(tm, zb.shape[1]), _row_spec(tm, zc.shape[1]),
                  wgt(wa), wgt(wb), wgt(wc), gate(0), gate(1), gate(2)],
        out_specs=_tile_spec(tm, tn),
        out_shape=jax.ShapeDtypeStruct((t, d), BF16),
        scratch_shapes=[buf for a in (wa, wb, wc) for buf in _w16_scratch(a.shape[1], tn)],
        compiler_params=_params(2),
        name="merge_branches",
    )(za, zb, zc, wa, wb, wc, gates, gates, gates)


def _out_proj_kernel(a_ref, w_ref, x_ref, g_ref, *rest, next_mod):
    if next_mod:
        cb_ref, wm_ref, bm_ref, o_ref, mod_ref, w_a, w_b = rest
    else:
        o_ref, w_a, w_b = rest

    def compute(w16):
        o_ref[...] = x_ref[...] + g_ref[...] * _dot(a_ref[...], w16[0][...])
        if next_mod:
            mod_ref[...] = _ada_block(cb_ref, wm_ref, bm_ref)

    _run_streamed(((w_ref, w_a, w_b),), compute)


def _out_proj(a, w, layer, x, gate, ada=None):
    t, kdim = a.shape
    d = w.shape[-1]
    tm, tn = min(1024, t), min(512, d)
    nb = d // tn
    n_inner = t // tm
    in_specs = [_row_spec(tm, kdim), _stream_weight_spec(layer, kdim, n_inner, tn, nb, lambda j: j),
                _tile_spec(tm, tn), _col_spec(1, tn)]
    out_specs = [_tile_spec(tm, tn)]
    out_shape = [jax.ShapeDtypeStruct((t, d), F32)]
    args = [a, w, x, gate]
    if ada is not None:
        c, w_ada, b_ada = ada
        _, dm, n = w_ada.shape
        cols = n // (nb * n_inner)
        assert cols * nb * n_inner == n and cols % LANES == 0

        def block(jj, i):
            return jnp.where(jj == 0, 0, (jj - 1) * n_inner + i)

        in_specs += [pl.BlockSpec((dm, LANES), lambda jj, i: (0, 0)),
                     pl.BlockSpec((None, dm, cols), lambda jj, i: (layer + 1, 0, block(jj, i))),
                     pl.BlockSpec((None, 1, cols), lambda jj, i: (layer + 1, 0, block(jj, i)))]
        out_specs.append(pl.BlockSpec((1, cols), lambda jj, i: (0, block(jj, i))))
        out_shape.append(jax.ShapeDtypeStruct((1, n), F32))
        args += [_lane_broadcast(c), w_ada, b_ada.reshape(-1, 1, n)]
    outs = pl.pallas_call(
        functools.partial(_out_proj_kernel, next_mod=ada is not None),
        grid=(nb + 1, n_inner),
        in_specs=in_specs,
        out_specs=out_specs,
        out_shape=out_shape,
        scratch_shapes=_w16_scratch(kdim, tn),
        compiler_params=_params(2),
        name="out_proj_resid",
    )(*args)
    return outs if ada is not None else outs[0]


def _resid_kernel(a_ref, w_ref, x_ref, g_ref, o_ref, acc_ref):
    kk = pl.program_id(2)
    last = pl.num_programs(2) - 1

    @pl.when(kk == 0)
    def _():
        acc_ref[...] = _dot(a_ref[...], w_ref[...])

    @pl.when((kk > 0) & (kk < last))
    def _():
        acc_ref[...] += _dot(a_ref[...], w_ref[...])

    @pl.when(kk == last)
    def _():
        o_ref[...] = x_ref[...] + g_ref[...] * (acc_ref[...] + _dot(a_ref[...], w_ref[...]))


def _resid_matmul(a, w, x, gate, name):
    t, kdim = a.shape
    d = w.shape[1]
    tm, tn, tk = min(1024, t), min(1024, d), min(4096, kdim // 2)
    return pl.pallas_call(
        _resid_kernel,
        grid=(d // tn, t // tm, kdim // tk),
        in_specs=[pl.BlockSpec((tm, tk), lambda j, i, k: (i, k)),
                  pl.BlockSpec((tk, tn), lambda j, i, k: (k, j)),
                  pl.BlockSpec((tm, tn), lambda j, i, k: (i, j)),
                  pl.BlockSpec((1, tn), lambda j, i, k: (0, j))],
        out_specs=pl.BlockSpec((tm, tn), lambda j, i, k: (i, j)),
        out_shape=jax.ShapeDtypeStruct((t, d), F32),
        scratch_shapes=[pltpu.VMEM((tm, tn), F32)],
        compiler_params=_params(3),
        name=name,
    )(a, w, x, gate)


def kernel(x, c, positions, w_ada, b_ada, g_mix, w_in, conv_a, conv_b, conv_b_bias, ln_cf_g, ln_cf_b,
           w_out_a, w_out_b, w_out_c, w_o, g_mlp, w_mlp1, w_mlp2, g_final):
    b, t, d = x.shape
    assert b == 1, "kernel written for a single sequence"
    depth = w_ada.shape[0]
    d_ff = w_mlp1.shape[-1]
    xs = x[0]

    mod = _ada(c, w_ada, b_ada, 0)
    tables = _rope_tables(positions[0])

    col_sc = 0
    col_cf = col_sc + 3 * SC_WIDTH
    col_q = col_cf + 2 * CF_WIDTH
    col_gate = col_q + 3 * DW_WIDTH

    for l in range(depth):
        shift1, scale1, gate1, shift2, scale2, gate2 = (mod[:, i * d:(i + 1) * d] for i in range(N_MOD))

        h = _norm_mod(xs, g_mix[l][None, :], scale1, shift1)
        z_a = _proj_sconv(h, w_in, l, conv_a[l], col_sc)
        u = _proj_glu(h, w_in, l, col_cf)

        outs, lses = [], []
        for g, (window, dilation) in enumerate(DW_PATTERNS):
            assert window // dilation == ATTN_BLOCK
            o_g, lse_g = _attention_group(_proj_qkv(h, w_in, l, tables, col_q, g, dilation))
            outs.append(o_g)
            lses.append(lse_g)
        z_c = _combine(outs, lses)

        z_b = _conformer(u, conv_b[l], conv_b_bias[l][None, :], ln_cf_g[l][None, :], ln_cf_b[l][None, :])
        gates = _proj_act(h, w_in, l, col_gate, 3 * d, "sigmoid", "proj_gates")
        merged = _merge(z_a, z_b, z_c, w_out_a, w_out_b, w_out_c, l, gates)
        if l + 1 < depth:
            xs, mod = _out_proj(merged, w_o, l, xs, gate1, ada=(c, w_ada, b_ada))
        else:
            xs = _out_proj(merged, w_o, l, xs, gate1)

        h2 = _norm_mod(xs, g_mlp[l][None, :], scale2, shift2)
        a, w_down = _proj_act(h2, w_mlp1, l, 0, d_ff, "relu2", "mlp_up", side=w_mlp2)
        xs = _resid_matmul(a, w_down, xs, gate2, "mlp_down_resid")

    return _final_norm(xs, g_final[None, :])[None]
```

```python
import functools

import jax
import jax.numpy as jnp
from jax import lax
from jax.experimental import pallas as pl
from jax.experimental.pallas import tpu as pltpu

F32 = jnp.float32
BF16 = jnp.bfloat16

EPS = 1e-6
SC_WIDTH = 2048
SC_KERNEL = 3
CF_WIDTH = 2048
CF_KERNEL = 31
HEAD_DIM = 128
DW_PATTERNS = ((128, 1), (512, 4), (2048, 16))
N_DW_GROUPS = len(DW_PATTERNS)
DW_HEADS = 8
DW_GROUP_WIDTH = DW_HEADS * HEAD_DIM
DW_WIDTH = N_DW_GROUPS * DW_GROUP_WIDTH
ATTN_BLOCK = 128
ROT_DIM = HEAD_DIM // 4
ROT_HALF = ROT_DIM // 2
ROPE_THETA = 500000.0
N_MOD = 6

LANES = 128
SUBLANES = 8
MXU_COLS = 256
CF_HALO = 32
VMEM_LIMIT = 61 * 1024 * 1024


def _params(n_axes):
    return pltpu.CompilerParams(dimension_semantics=("arbitrary",) * n_axes,
                                vmem_limit_bytes=VMEM_LIMIT)


def _dot(a, b):
    return jnp.dot(a, b, preferred_element_type=F32)


def _sigmoid(v):
    return jax.nn.sigmoid(v)


def _stream_weight_spec(layer, k_rows, n_chunks, tn, n_col_tiles, col_block):
    chunk = k_rows // n_chunks
    assert chunk * n_chunks == k_rows and chunk % 16 == 0

    def imap(jj, i):
        return (layer, jnp.where(jj < n_col_tiles, i, 0), col_block(jnp.minimum(jj, n_col_tiles - 1)))

    return pl.BlockSpec((None, chunk, tn), imap)


def _row_spec(tm, width):
    return pl.BlockSpec((tm, width), lambda jj, i: (jnp.where(jj == 0, 0, i), 0))


def _tile_spec(tm, tn, col_block=lambda j: j):
    return pl.BlockSpec((tm, tn), lambda jj, i: (jnp.where(jj == 0, 0, i), col_block(jnp.maximum(jj - 1, 0))))


def _col_spec(rows, tn):
    return pl.BlockSpec((rows, tn), lambda jj, i: (0, jnp.maximum(jj - 1, 0)))


def _stream_cast(pairs):
    for w_ref, w16 in pairs:
        rows = w_ref.shape[0]
        r0 = pl.multiple_of(pl.program_id(1) * rows, rows)
        w16[pl.ds(r0, rows), :] = w_ref[...].astype(BF16)


def _run_streamed(streams, compute):
    jj = pl.program_id(0)
    fill_a = [(w, a) for w, a, _ in streams]
    fill_b = [(w, b) for w, _, b in streams]

    @pl.when(jj == 0)
    def _():
        _stream_cast(fill_a)

    @pl.when(jj % 2 == 1)
    def _():
        _stream_cast(fill_b)
        compute([a for _, a, _ in streams])

    @pl.when((jj > 0) & (jj % 2 == 0))
    def _():
        _stream_cast(fill_a)
        compute([b for _, _, b in streams])


def _w16_scratch(k_rows, tn):
    return [pltpu.VMEM((k_rows, tn), BF16)] * 2


def _ada_block(cb_ref, w_ref, b_ref):
    cb = cb_ref[...]
    c_act = cb * _sigmoid(cb)
    parts = [jnp.sum(w_ref[:, j:j + LANES] * c_act, axis=0, keepdims=True)
             for j in range(0, w_ref.shape[1], LANES)]
    return jnp.concatenate(parts, axis=1) + b_ref[...]


def _ada_kernel(cb_ref, w_ref, b_ref, o_ref):
    o_ref[...] = _ada_block(cb_ref, w_ref, b_ref)


def _lane_broadcast(c):
    return jnp.broadcast_to(c.reshape(-1, 1), (c.size, LANES))


def _ada(c, w_ada, b_ada, layer):
    _, d, n = w_ada.shape
    tn = min(512, n)
    return pl.pallas_call(
        _ada_kernel,
        grid=(n // tn,),
        in_specs=[pl.BlockSpec((d, LANES), lambda j: (0, 0)),
                  pl.BlockSpec((None, d, tn), lambda j: (layer, 0, j)),
                  pl.BlockSpec((None, 1, tn), lambda j: (layer, 0, j))],
        out_specs=pl.BlockSpec((1, tn), lambda j: (0, j)),
        out_shape=jax.ShapeDtypeStruct((1, n), F32),
        compiler_params=_params(1),
        name="ada_mod",
    )(_lane_broadcast(c), w_ada, b_ada.reshape(-1, 1, n))


def _rms(x):
    return x * lax.rsqrt(jnp.mean(x * x, axis=-1, keepdims=True) + EPS)


def _norm_mod_kernel(x_ref, g_ref, sc_ref, sh_ref, o_ref):
    y = _rms(x_ref[...]) * g_ref[...]
    o_ref[...] = (y * (1.0 + sc_ref[...]) + sh_ref[...]).astype(o_ref.dtype)


def _norm_mod(x, g, scale, shift):
    t, d = x.shape
    tm = min(512, t)
    row = pl.BlockSpec((1, d), lambda i: (0, 0))
    return pl.pallas_call(
        _norm_mod_kernel,
        grid=(t // tm,),
        in_specs=[pl.BlockSpec((tm, d), lambda i: (i, 0)), row, row, row],
        out_specs=pl.BlockSpec((tm, d), lambda i: (i, 0)),
        out_shape=jax.ShapeDtypeStruct((t, d), BF16),
        compiler_params=_params(1),
        name="norm_mod",
    )(x, g, scale, shift)


def _final_norm_kernel(x_ref, g_ref, o_ref):
    o_ref[...] = _rms(x_ref[...]) * g_ref[...]


def _final_norm(x, g):
    t, d = x.shape
    tm = min(512, t)
    return pl.pallas_call(
        _final_norm_kernel,
        grid=(t // tm,),
        in_specs=[pl.BlockSpec((tm, d), lambda i: (i, 0)), pl.BlockSpec((1, d), lambda i: (0, 0))],
        out_specs=pl.BlockSpec((tm, d), lambda i: (i, 0)),
        out_shape=jax.ShapeDtypeStruct((t, d), F32),
        compiler_params=_params(1),
        name="final_norm",
    )(x, g)


def _rope_table_kernel(pos_ref, f_ref, c_ref, sa_ref, sb_ref):
    ang = pos_ref[...] * f_ref[...]
    cos, sin = jnp.cos(ang), jnp.sin(ang)
    lane = lax.broadcasted_iota(jnp.int32, ang.shape, 1)
    c_ref[...] = jnp.where(lane < ROT_DIM, cos, 1.0)
    sa_ref[...] = jnp.where(lane < ROT_HALF, -sin, 0.0)
    sb_ref[...] = jnp.where((lane >= ROT_HALF) & (lane < ROT_DIM), sin, 0.0)


def _rope_tables(positions):
    t = positions.shape[-1]
    tm = min(1024, t)
    inv_freq = ROPE_THETA ** (-jnp.arange(0, ROT_DIM, 2, dtype=F32) / ROT_DIM)
    f_row = jnp.concatenate([inv_freq, inv_freq, jnp.zeros((LANES - ROT_DIM,), F32)])[None, :]
    pos = positions.astype(F32).reshape(t, 1)
    tab = jax.ShapeDtypeStruct((t, LANES), F32)
    blk = pl.BlockSpec((tm, LANES), lambda i: (i, 0))
    return pl.pallas_call(
        _rope_table_kernel,
        grid=(t // tm,),
        in_specs=[pl.BlockSpec((tm, 1), lambda i: (i, 0)), pl.BlockSpec((1, LANES), lambda i: (0, 0))],
        out_specs=[blk, blk, blk],
        out_shape=[tab, tab, tab],
        compiler_params=_params(1),
        name="rope_tables",
    )(pos, f_row)


def _proj_sconv_kernel(h_ref, wb_ref, wc_ref, wx_ref, cw_ref, o_ref, wb_a, wb_b, wc_a, wc_b, wx_a, wx_b, ext_ref):
    tm = h_ref.shape[0]

    def compute(w16):
        wb16, wc16, wx16 = w16

        @pl.when(pl.program_id(1) == 0)
        def _():
            ext_ref[0:SUBLANES, :] = jnp.zeros((SUBLANES, ext_ref.shape[1]), F32)

        h = h_ref[...]
        p = _dot(h, wc16[...]) * _dot(h, wx16[...])
        ext_ref[SUBLANES:, :] = p
        conv = (cw_ref[0:1, :] * ext_ref[pl.ds(SUBLANES - 2, tm), :]
                + cw_ref[1:2, :] * ext_ref[pl.ds(SUBLANES - 1, tm), :]
                + cw_ref[2:3, :] * p)
        o_ref[...] = (_dot(h, wb16[...]) * conv).astype(o_ref.dtype)
        ext_ref[0:SUBLANES, :] = ext_ref[tm:tm + SUBLANES, :]

    _run_streamed(((wb_ref, wb_a, wb_b), (wc_ref, wc_a, wc_b), (wx_ref, wx_a, wx_b)), compute)


def _proj_sconv(h, w_in, layer, conv_w, col0):
    t, d = h.shape
    tm, tn = min(1024, t), MXU_COLS
    nb = SC_WIDTH // tn
    n_inner = t // tm
    c0 = col0 // tn

    def wspec(sec):
        return _stream_weight_spec(layer, d, n_inner, tn, nb, lambda j: c0 + sec * nb + j)

    return pl.pallas_call(
        _proj_sconv_kernel,
        grid=(nb + 1, n_inner),
        in_specs=[_row_spec(tm, d), wspec(0), wspec(1), wspec(2), _col_spec(SC_KERNEL, tn)],
        out_specs=_tile_spec(tm, tn),
        out_shape=jax.ShapeDtypeStruct((t, SC_WIDTH), BF16),
        scratch_shapes=_w16_scratch(d, tn) * 3 + [pltpu.VMEM((tm + SUBLANES, tn), F32)],
        compiler_params=_params(2),
        name="proj_sconv",
    )(h, w_in, w_in, w_in, conv_w)


def _proj_glu_kernel(h_ref, wa_ref, wg_ref, o_ref, wa_a, wa_b, wg_a, wg_b):
    def compute(w16):
        wa16, wg16 = w16
        h = h_ref[...]
        for c0 in range(0, o_ref.shape[1], MXU_COLS):
            cs = slice(c0, c0 + MXU_COLS)
            o_ref[:, cs] = _dot(h, wa16[:, cs]) * _sigmoid(_dot(h, wg16[:, cs]))

    _run_streamed(((wa_ref, wa_a, wa_b), (wg_ref, wg_a, wg_b)), compute)


def _proj_glu(h, w_in, layer, col0):
    t, d = h.shape
    tm, tn = min(1024, t), 512
    nb = CF_WIDTH // tn
    n_inner = t // tm
    c0 = col0 // tn
    return pl.pallas_call(
        _proj_glu_kernel,
        grid=(nb + 1, n_inner),
        in_specs=[_row_spec(tm, d),
                  _stream_weight_spec(layer, d, n_inner, tn, nb, lambda j: c0 + j),
                  _stream_weight_spec(layer, d, n_inner, tn, nb, lambda j: c0 + nb + j)],
        out_specs=_tile_spec(tm, tn),
        out_shape=jax.ShapeDtypeStruct((t, CF_WIDTH), F32),
        scratch_shapes=_w16_scratch(d, tn) * 2,
        compiler_params=_params(2),
        name="proj_glu",
    )(h, w_in, w_in)


def _proj_qkv_kernel(h_ref, w_ref, c_ref, sa_ref, sb_ref, o_ref, w_a, w_b, slab_ref, *, dilation):
    tm = h_ref.shape[0]
    tn = w_a.shape[1]
    half = MXU_COLS
    rows = tm // dilation

    def compute(w16, rope):
        c, sa, sb = c_ref[...], sa_ref[...], sb_ref[...]
        h = h_ref[...]
        for c0 in range(0, tn, half):
            acc = _dot(h, w16[:, c0:c0 + half])
            for s in range(half // LANES):
                y = acc[:, s * LANES:(s + 1) * LANES]
                if rope:
                    y = y * c + pltpu.roll(y, LANES - ROT_HALF, 1) * sa + pltpu.roll(y, ROT_HALF, 1) * sb
                lanes = slice(c0 + s * LANES, c0 + (s + 1) * LANES)
                if dilation == 1:
                    o_ref[0, :, lanes] = y.astype(o_ref.dtype)
                else:
                    slab = c0 // LANES + s
                    slab_ref[slab] = y
                    for r in range(dilation):
                        o_ref[r, :, lanes] = slab_ref[slab, pl.ds(r, rows, stride=dilation), :].astype(o_ref.dtype)

    jj = pl.program_id(0)
    for step, (use, fill, rope) in enumerate(((None, w_a, False), (w_a, w_b, True), (w_b, w_a, True),
                                              (w_a, w_b, False))):
        @pl.when(jj == step)
        def _(use=use, fill=fill, rope=rope):
            _stream_cast([(w_ref, fill)])
            if use is not None:
                compute(use, rope)


def _proj_qkv(h, w_in, layer, tables, col_q, group, dilation):
    t, d = h.shape
    tm, tn = min(1024, t), DW_GROUP_WIDTH
    n_inner = t // tm
    c0 = col_q // tn + group
    sec_stride = DW_WIDTH // tn
    tab = _row_spec(tm, LANES)
    return pl.pallas_call(
        functools.partial(_proj_qkv_kernel, dilation=dilation),
        grid=(3 + 1, n_inner),
        in_specs=[_row_spec(tm, d),
                  _stream_weight_spec(layer, d, n_inner, tn, 3, lambda j: c0 + j * sec_stride),
                  tab, tab, tab],
        out_specs=pl.BlockSpec((None, dilation, tm // dilation, tn),
                               lambda jj, i: (jnp.maximum(jj - 1, 0), 0, jnp.where(jj == 0, 0, i), 0)),
        out_shape=jax.ShapeDtypeStruct((3, dilation, t // dilation, tn), BF16),
        scratch_shapes=_w16_scratch(d, tn) + [pltpu.VMEM((tn // LANES, tm, LANES), F32)],
        compiler_params=_params(2),
        name=f"proj_qkv_d{dilation}",
    )(h, w_in, *tables)


def _activate(acc, act):
    if act == "sigmoid":
        return _sigmoid(acc)
    if act == "relu2":
        return jnp.square(jnp.maximum(acc, 0.0))
    raise ValueError(act)


def _proj_act_kernel(h_ref, w_ref, *rest, act, cast_side, next_mod):
    rest = list(rest)
    side_ref = rest.pop(0) if cast_side else None
    cb_ref, wm_ref, bm_ref = (rest.pop(0), rest.pop(0), rest.pop(0)) if next_mod else (None, None, None)
    o_ref = rest.pop(0)
    side16_ref = rest.pop(0) if cast_side else None
    mod_ref = rest.pop(0) if next_mod else None
    w_a, w_b = rest
    tn = o_ref.shape[1]

    def compute(w16):
        h = h_ref[...]
        for c0 in range(0, tn, MXU_COLS):
            cs = slice(c0, c0 + MXU_COLS)
            o_ref[:, cs] = _activate(_dot(h, w16[0][:, cs]), act).astype(o_ref.dtype)
        if cast_side:
            side16_ref[...] = side_ref[...].astype(BF16)
        if next_mod:
            mod_ref[...] = _ada_block(cb_ref, wm_ref, bm_ref)

    _run_streamed(((w_ref, w_a, w_b),), compute)


def _proj_act(h, w, layer, col0, width, act, name, side=None, ada=None):
    t, d = h.shape
    tm = min(1024, t)
    tn = 1024 if width % 1024 == 0 and col0 % 1024 == 0 else 512
    nb = width // tn
    n_inner = t // tm
    n_steps = nb * n_inner
    c0 = col0 // tn

    def step(jj, i):
        return jnp.where(jj == 0, 0, (jj - 1) * n_inner + i)

    in_specs = [_row_spec(tm, d), _stream_weight_spec(layer, d, n_inner, tn, nb, lambda j: c0 + j)]
    out_specs = [_tile_spec(tm, tn)]
    out_shape = [jax.ShapeDtypeStruct((t, width), BF16)]
    args = [h, w]
    if side is not None:
        _, side_rows, side_cols = side.shape
        blk = side_rows // n_steps
        assert blk * n_steps == side_rows and blk % 16 == 0
        in_specs.append(pl.BlockSpec((None, blk, side_cols), lambda jj, i: (layer, step(jj, i), 0)))
        args.append(side)
    if ada is not None:
        c, w_ada, b_ada = ada
        _, dm, n = w_ada.shape
        cols = n // n_steps
        assert cols * n_steps == n and cols % LANES == 0
        in_specs += [pl.BlockSpec((dm, LANES), lambda jj, i: (0, 0)),
                     pl.BlockSpec((None, dm, cols), lambda jj, i: (layer + 1, 0, step(jj, i))),
                     pl.BlockSpec((None, 1, cols), lambda jj, i: (layer + 1, 0, step(jj, i)))]
        args += [_lane_broadcast(c), w_ada, b_ada.reshape(-1, 1, n)]
    if side is not None:
        out_specs.append(pl.BlockSpec((blk, side_cols), lambda jj, i: (step(jj, i), 0)))
        out_shape.append(jax.ShapeDtypeStruct((side_rows, side_cols), BF16))
    if ada is not None:
        out_specs.append(pl.BlockSpec((1, cols), lambda jj, i: (0, step(jj, i))))
        out_shape.append(jax.ShapeDtypeStruct((1, n), F32))
    outs = pl.pallas_call(
        functools.partial(_proj_act_kernel, act=act, cast_side=side is not None, next_mod=ada is not None),
        grid=(nb + 1, n_inner),
        in_specs=in_specs,
        out_specs=out_specs,
        out_shape=out_shape,
        scratch_shapes=_w16_scratch(d, tn),
        compiler_params=_params(2),
        name=name,
    )(*args)
    return outs if len(outs) > 1 else outs[0]


CF_ROWS = 64
CF_COLS = 256


def _conformer_tile(is_first, uh_ref, u_ref, w_ref, b_ref, g_ref, be_ref, o_ref, sh_ref, y_ref):
    tm, width = u_ref.shape
    n_ext = tm + CF_HALO
    sh_ref[0, 0:CF_HALO, :] = jnp.where(is_first, 0.0, uh_ref[...])
    sh_ref[0, CF_HALO:, :] = u_ref[...]
    for s in range(1, SUBLANES):
        sh_ref[s, 0:n_ext - SUBLANES, :] = sh_ref[0, pl.ds(s, n_ext - SUBLANES), :]
    first = CF_HALO - (CF_KERNEL - 1)

    def row_chunk(rc, carry):
        r0 = pl.multiple_of(rc * CF_ROWS, CF_ROWS)
        for c0 in range(0, width, CF_COLS):
            cs = slice(c0, c0 + CF_COLS)
            acc = jnp.broadcast_to(b_ref[:, cs], (CF_ROWS, CF_COLS))
            for k in range(CF_KERNEL):
                shift = (first + k) % SUBLANES
                base = pl.multiple_of(r0 + (first + k - shift), SUBLANES)
                acc = acc + w_ref[k:k + 1, cs] * sh_ref[shift, pl.ds(base, CF_ROWS), cs]
            y_ref[pl.ds(r0, CF_ROWS), cs] = acc
        return carry

    lax.fori_loop(0, tm // CF_ROWS, row_chunk, 0)
    y = y_ref[...]
    mu = jnp.mean(y, axis=-1, keepdims=True)
    yc = y - mu
    var = jnp.mean(yc * yc, axis=-1, keepdims=True)
    z = yc * lax.rsqrt(var + EPS) * g_ref[...] + be_ref[...]
    o_ref[...] = (z * _sigmoid(z)).astype(o_ref.dtype)


def _conformer_kernel(uh_ref, u_ref, w_ref, b_ref, g_ref, be_ref, o_ref, sh_ref, y_ref):
    _conformer_tile(pl.program_id(0) == 0, uh_ref, u_ref, w_ref, b_ref, g_ref, be_ref, o_ref, sh_ref, y_ref)


def _conformer(u, conv_w, bias, ln_g, ln_b):
    t, width = u.shape
    tm = min(256, t)
    per = tm // CF_HALO
    row = pl.BlockSpec((1, width), lambda i: (0, 0))
    return pl.pallas_call(
        _conformer_kernel,
        grid=(t // tm,),
        in_specs=[pl.BlockSpec((CF_HALO, width), lambda i: (jnp.maximum(i * per - 1, 0), 0)),
                  pl.BlockSpec((tm, width), lambda i: (i, 0)),
                  pl.BlockSpec((CF_KERNEL, width), lambda i: (0, 0)), row, row, row],
        out_specs=pl.BlockSpec((tm, width), lambda i: (i, 0)),
        out_shape=jax.ShapeDtypeStruct((t, width), BF16),
        scratch_shapes=[pltpu.VMEM((SUBLANES, tm + CF_HALO, width), F32), pltpu.VMEM((tm, width), F32)],
        compiler_params=_params(1),
        name="conformer_conv",
    )(u, u, conv_w, bias, ln_g, ln_b)


def _attn_kernel(q_ref, kp_ref, kc_ref, vp_ref, vc_ref, o_ref, l_ref):
    n = ATTN_BLOCK
    n_blocks = q_ref.shape[0] // n
    k = jnp.concatenate([kp_ref[...], kc_ref[...]], axis=0)
    v = jnp.concatenate([vp_ref[...], vc_ref[...]], axis=0)
    qi = lax.broadcasted_iota(jnp.int32, (n, 2 * n), 0)
    kj = lax.broadcasted_iota(jnp.int32, (n, 2 * n), 1)
    band = (kj >= qi) & (kj <= qi + n)
    has_prev = pl.program_id(1) > 0
    l_ref[...] = jnp.zeros(l_ref.shape, F32)
    for jb in range(n_blocks):
        rows = slice(jb * n, (jb + 1) * n)
        keys = slice(jb * n, (jb + 2) * n)
        valid = band & ((kj >= n) | has_prev) if jb == 0 else band
        for h in range(DW_HEADS):
            sl = slice(h * HEAD_DIM, (h + 1) * HEAD_DIM)
            s = lax.dot_general(q_ref[rows, sl], k[keys, sl], (((1,), (1,)), ((), ())),
                                preferred_element_type=F32) * (HEAD_DIM ** -0.5)
            s = jnp.where(valid, s, -jnp.inf)
            m = jnp.max(s, axis=-1, keepdims=True)
            p = jnp.exp(s - m)
            den = jnp.sum(p, axis=-1, keepdims=True)
            o_ref[rows, sl] = _dot((p / den).astype(BF16), v[keys, sl])
            l_ref[rows, h:h + 1] = m + jnp.log(den)


def _attention_group(qkv):
    _, dilation, sub, gw = qkv.shape
    n = ATTN_BLOCK
    assert sub % n == 0
    per_step = min(4, sub // n)
    tq = per_step * n
    assert sub % tq == 0

    def cur(sec):
        return pl.BlockSpec((None, None, tq, gw), lambda r, b: (sec, r, b, 0))

    def prev(sec):
        return pl.BlockSpec((None, None, n, gw), lambda r, b: (sec, r, jnp.maximum(b * per_step - 1, 0), 0))

    return pl.pallas_call(
        _attn_kernel,
        grid=(dilation, sub // tq),
        in_specs=[cur(0), prev(1), cur(1), prev(2), cur(2)],
        out_specs=[pl.BlockSpec((None, tq, gw), lambda r, b: (r, b, 0)),
                   pl.BlockSpec((None, tq, LANES), lambda r, b: (r, b, 0))],
        out_shape=[jax.ShapeDtypeStruct((dilation, sub, gw), F32),
                   jax.ShapeDtypeStruct((dilation, sub, LANES), F32)],
        compiler_params=_params(2),
        name=f"dilated_attn_d{dilation}",
    )(qkv, qkv, qkv, qkv, qkv)


def _combine_kernel(o0_ref, o1_ref, o2_ref, l0_ref, l1_ref, l2_ref, out_ref, lnat_ref, onat_ref):
    o_refs = (o0_ref, o1_ref, o2_ref)
    l_refs = (l0_ref, l1_ref, l2_ref)
    tm = out_ref.shape[0]

    def natural(dst_ref, src, dilation):
        rows = tm // dilation
        for r in range(dilation):
            dst_ref[pl.ds(r, rows, stride=dilation), :] = src(r)

    lse = []
    for g, l_ref in enumerate(l_refs):
        dilation = l_ref.shape[0]
        if dilation == 1:
            lse.append(l_ref[0])
        else:
            natural(lnat_ref.at[g], lambda r, l_ref=l_ref: l_ref[r], dilation)
            lse.append(lnat_ref[g])
    m = jnp.maximum(jnp.maximum(lse[0], lse[1]), lse[2])
    e = [jnp.exp(l - m) for l in lse]
    den = e[0] + e[1] + e[2]
    w = [x / den for x in e]
    for h in range(DW_HEADS):
        sl = slice(h * HEAD_DIM, (h + 1) * HEAD_DIM)
        acc = None
        for g, o_ref in enumerate(o_refs):
            dilation = o_ref.shape[0]
            if dilation == 1:
                o_nat = o_ref[0, :, sl]
            else:
                natural(onat_ref, lambda r, o_ref=o_ref: o_ref[r, :, sl], dilation)
                o_nat = onat_ref[...]
            term = w[g][:, h:h + 1] * o_nat
            acc = term if acc is None else acc + term
        out_ref[:, sl] = acc.astype(out_ref.dtype)


def _combine(outs, lses):
    gw = outs[0].shape[-1]
    t = outs[0].shape[0] * outs[0].shape[1]
    tm = min(512, t)

    def spec(a):
        dil = a.shape[0]
        return pl.BlockSpec((dil, tm // dil, a.shape[-1]), lambda i: (0, i, 0))

    return pl.pallas_call(
        _combine_kernel,
        grid=(t // tm,),
        in_specs=[spec(a) for a in outs] + [spec(a) for a in lses],
        out_specs=pl.BlockSpec((tm, gw), lambda i: (i, 0)),
        out_shape=jax.ShapeDtypeStruct((t, gw), BF16),
        scratch_shapes=[pltpu.VMEM((N_DW_GROUPS, tm, LANES), F32), pltpu.VMEM((tm, LANES), F32)],
        compiler_params=_params(1),
        name="attn_combine",
    )(*outs, *lses)


def _merge_kernel(za_ref, zb_ref, zc_ref, wa_ref, wb_ref, wc_ref, ga_ref, gb_ref, gc_ref, o_ref,
                  wa_a, wa_b, wb_a, wb_b, wc_a, wc_b):
    def compute(w16):
        wa16, wb16, wc16 = w16
        m = ga_ref[...].astype(F32) * _dot(za_ref[...], wa16[...])
        m = m + gb_ref[...].astype(F32) * _dot(zb_ref[...], wb16[...])
        m = m + gc_ref[...].astype(F32) * _dot(zc_ref[...], wc16[...])
        o_ref[...] = m.astype(o_ref.dtype)

    _run_streamed(((wa_ref, wa_a, wa_b), (wb_ref, wb_a, wb_b), (wc_ref, wc_a, wc_b)), compute)


def _merge(za, zb, zc, wa, wb, wc, layer, gates):
    t = za.shape[0]
    d = wa.shape[-1]
    tm, tn = min(1024, t), min(512, d)
    nb = d // tn
    n_inner = t // tm

    def wgt(a):
        return _stream_weight_spec(layer, a.shape[1], n_inner, tn, nb, lambda j: j)

    def gate(sec):
        return _tile_spec(tm, tn, lambda j: sec * nb + j)

    return pl.pallas_call(
        _merge_kernel,
        grid=(nb + 1, n_inner),
        in_specs=[_row_spec(tm, za.shape[1]), _row_spec(tm, zb.shape[1]), _row_spec(tm, zc.shape[1]),
                  wgt(wa), wgt(wb), wgt(wc), gate(0), gate(1), gate(2)],
        out_specs=_tile_spec(tm, tn),
        out_shape=jax.ShapeDtypeStruct((t, d), BF16),
        scratch_shapes=[buf for a in (wa, wb, wc) for buf in _w16_scratch(a.shape[1], tn)],
        compiler_params=_params(2),
        name="merge_branches",
    )(za, zb, zc, wa, wb, wc, gates, gates, gates)


def _out_proj_kernel(a_ref, w_ref, x_ref, g_ref, o_ref, w_a, w_b):
    def compute(w16):
        o_ref[...] = x_ref[...] + g_ref[...] * _dot(a_ref[...], w16[0][...])

    _run_streamed(((w_ref, w_a, w_b),), compute)


def _out_proj(a, w, layer, x, gate):
    t, kdim = a.shape
    d = w.shape[-1]
    tm, tn = min(1024, t), min(1024, d)
    nb = d // tn
    n_inner = t // tm
    return pl.pallas_call(
        _out_proj_kernel,
        grid=(nb + 1, n_inner),
        in_specs=[_row_spec(tm, kdim), _stream_weight_spec(layer, kdim, n_inner, tn, nb, lambda j: j),
                  _tile_spec(tm, tn), _col_spec(1, tn)],
        out_specs=_tile_spec(tm, tn),
        out_shape=jax.ShapeDtypeStruct((t, d), F32),
        scratch_shapes=_w16_scratch(kdim, tn),
        compiler_params=_params(2),
        name="out_proj_resid",
    )(a, w, x, gate)


def _resid_kernel(a_ref, w_ref, x_ref, g_ref, o_ref, acc_ref):
    kk = pl.program_id(2)
    last = pl.num_programs(2) - 1

    @pl.when(kk == 0)
    def _():
        acc_ref[...] = _dot(a_ref[...], w_ref[...])

    @pl.when((kk > 0) & (kk < last))
    def _():
        acc_ref[...] += _dot(a_ref[...], w_ref[...])

    @pl.when(kk == last)
    def _():
        o_ref[...] = x_ref[...] + g_ref[...] * (acc_ref[...] + _dot(a_ref[...], w_ref[...]))


def _resid_matmul(a, w, x, gate, name):
    t, kdim = a.shape
    d = w.shape[1]
    tm, tn, tk = min(1024, t), min(1024, d), min(4096, kdim // 2)
    return pl.pallas_call(
        _resid_kernel,
        grid=(d // tn, t // tm, kdim // tk),
        in_specs=[pl.BlockSpec((tm, tk), lambda j, i, k: (i, k)),
                  pl.BlockSpec((tk, tn), lambda j, i, k: (k, j)),
                  pl.BlockSpec((tm, tn), lambda j, i, k: (i, j)),
                  pl.BlockSpec((1, tn), lambda j, i, k: (0, j))],
        out_specs=pl.BlockSpec((tm, tn), lambda j, i, k: (i, j)),
        out_shape=jax.ShapeDtypeStruct((t, d), F32),
        scratch_shapes=[pltpu.VMEM((tm, tn), F32)],
        compiler_params=_params(3),
        name=name,
    )(a, w, x, gate)


def kernel(x, c, positions, w_ada, b_ada, g_mix, w_in, conv_a, conv_b, conv_b_bias, ln_cf_g, ln_cf_b,
           w_out_a, w_out_b, w_out_c, w_o, g_mlp, w_mlp1, w_mlp2, g_final):
    b, t, d = x.shape
    assert b == 1, "kernel written for a single sequence"
    depth = w_ada.shape[0]
    d_ff = w_mlp1.shape[-1]
    xs = x[0]

    mod = _ada(c, w_ada, b_ada, 0)
    tables = _rope_tables(positions[0])

    col_sc = 0
    col_cf = col_sc + 3 * SC_WIDTH
    col_q = col_cf + 2 * CF_WIDTH
    col_gate = col_q + 3 * DW_WIDTH

    for l in range(depth):
        shift1, scale1, gate1, shift2, scale2, gate2 = (mod[:, i * d:(i + 1) * d] for i in range(N_MOD))

        h = _norm_mod(xs, g_mix[l][None, :], scale1, shift1)
        z_a = _proj_sconv(h, w_in, l, conv_a[l], col_sc)
        u = _proj_glu(h, w_in, l, col_cf)

        outs, lses = [], []
        for g, (window, dilation) in enumerate(DW_PATTERNS):
            assert window // dilation == ATTN_BLOCK
            o_g, lse_g = _attention_group(_proj_qkv(h, w_in, l, tables, col_q, g, dilation))
            outs.append(o_g)
            lses.append(lse_g)
        z_c = _combine(outs, lses)

        z_b = _conformer(u, conv_b[l], conv_b_bias[l][None, :], ln_cf_g[l][None, :], ln_cf_b[l][None, :])
        if l + 1 < depth:
            gates, next_mod = _proj_act(h, w_in, l, col_gate, 3 * d, "sigmoid", "proj_gates",
                                        ada=(c, w_ada, b_ada))
        else:
            gates = _proj_act(h, w_in, l, col_gate, 3 * d, "sigmoid", "proj_gates")
        merged = _merge(z_a, z_b, z_c, w_out_a, w_out_b, w_out_c, l, gates)
        xs = _out_proj(merged, w_o, l, xs, gate1)

        h2 = _norm_mod(xs, g_mlp[l][None, :], scale2, shift2)
        a, w_down = _proj_act(h2, w_mlp1, l, 0, d_ff, "relu2", "mlp_up", side=w_mlp2)
        xs = _resid_matmul(a, w_down, xs, gate2, "mlp_down_resid")
        if l + 1 < depth:
            mod = next_mod

    return _final_norm(xs, g_final[None, :])[None]
```

```python
import functools

import jax
import jax.numpy as jnp
from jax import lax
from jax.experimental import pallas as pl
from jax.experimental.pallas import tpu as pltpu

F32 = jnp.float32
BF16 = jnp.bfloat16

EPS = 1e-6
SC_WIDTH = 2048
SC_KERNEL = 3
CF_WIDTH = 2048
CF_KERNEL = 31
HEAD_DIM = 128
DW_PATTERNS = ((128, 1), (512, 4), (2048, 16))
N_DW_GROUPS = len(DW_PATTERNS)
DW_HEADS = 8
DW_GROUP_WIDTH = DW_HEADS * HEAD_DIM
DW_WIDTH = N_DW_GROUPS * DW_GROUP_WIDTH
ATTN_BLOCK = 128
ROT_DIM = HEAD_DIM // 4
ROT_HALF = ROT_DIM // 2
ROPE_THETA = 500000.0
N_MOD = 6

LANES = 128
SUBLANES = 8
MXU_COLS = 256
CF_HALO = 32
VMEM_LIMIT = 61 * 1024 * 1024


def _params(n_axes):
    return pltpu.CompilerParams(dimension_semantics=("arbitrary",) * n_axes,
                                vmem_limit_bytes=VMEM_LIMIT)


def _dot(a, b):
    return jnp.dot(a, b, preferred_element_type=F32)


def _sigmoid(v):
    return jax.nn.sigmoid(v)


def _stream_weight_spec(layer, k_rows, n_chunks, tn, n_col_tiles, col_block):
    chunk = k_rows // n_chunks
    assert chunk * n_chunks == k_rows and chunk % 16 == 0

    def imap(jj, i):
        return (layer, jnp.where(jj < n_col_tiles, i, 0), col_block(jnp.minimum(jj, n_col_tiles - 1)))

    return pl.BlockSpec((None, chunk, tn), imap)


def _row_spec(tm, width):
    return pl.BlockSpec((tm, width), lambda jj, i: (jnp.where(jj == 0, 0, i), 0))


def _tile_spec(tm, tn, col_block=lambda j: j):
    return pl.BlockSpec((tm, tn), lambda jj, i: (jnp.where(jj == 0, 0, i), col_block(jnp.maximum(jj - 1, 0))))


def _col_spec(rows, tn):
    return pl.BlockSpec((rows, tn), lambda jj, i: (0, jnp.maximum(jj - 1, 0)))


def _stream_cast(pairs):
    for w_ref, w16 in pairs:
        rows = w_ref.shape[0]
        r0 = pl.multiple_of(pl.program_id(1) * rows, rows)
        w16[pl.ds(r0, rows), :] = w_ref[...].astype(BF16)


def _run_streamed(streams, compute):
    jj = pl.program_id(0)
    fill_a = [(w, a) for w, a, _ in streams]
    fill_b = [(w, b) for w, _, b in streams]

    @pl.when(jj == 0)
    def _():
        _stream_cast(fill_a)

    @pl.when(jj % 2 == 1)
    def _():
        _stream_cast(fill_b)
        compute([a for _, a, _ in streams])

    @pl.when((jj > 0) & (jj % 2 == 0))
    def _():
        _stream_cast(fill_a)
        compute([b for _, _, b in streams])


def _w16_scratch(k_rows, tn):
    return [pltpu.VMEM((k_rows, tn), BF16)] * 2


def _ada_block(cb_ref, w_ref, b_ref):
    c_act = _silu_ref(cb_ref)
    parts = [_ada_slab(c_act, w_ref, b_ref, j) for j in range(0, w_ref.shape[1], LANES)]
    return jnp.concatenate(parts, axis=1)


def _silu_ref(cb_ref):
    cb = cb_ref[...]
    return cb * _sigmoid(cb)


def _ada_slab(c_act, w_ref, b_ref, j):
    lanes = slice(j, j + LANES)
    return jnp.sum(w_ref[:, lanes] * c_act, axis=0, keepdims=True) + b_ref[:, lanes]


def _ada_kernel(cb_ref, w_ref, b_ref, o_ref):
    o_ref[...] = _ada_block(cb_ref, w_ref, b_ref)


def _lane_broadcast(c):
    return jnp.broadcast_to(c.reshape(-1, 1), (c.size, LANES))


def _ada(c, w_ada, b_ada, layer):
    _, d, n = w_ada.shape
    tn = min(512, n)
    return pl.pallas_call(
        _ada_kernel,
        grid=(n // tn,),
        in_specs=[pl.BlockSpec((d, LANES), lambda j: (0, 0)),
                  pl.BlockSpec((None, d, tn), lambda j: (layer, 0, j)),
                  pl.BlockSpec((None, 1, tn), lambda j: (layer, 0, j))],
        out_specs=pl.BlockSpec((1, tn), lambda j: (0, j)),
        out_shape=jax.ShapeDtypeStruct((1, n), F32),
        compiler_params=_params(1),
        name="ada_mod",
    )(_lane_broadcast(c), w_ada, b_ada.reshape(-1, 1, n))


def _rms(x):
    return x * lax.rsqrt(jnp.mean(x * x, axis=-1, keepdims=True) + EPS)


def _norm_mod_kernel(x_ref, g_ref, sc_ref, sh_ref, o_ref):
    y = _rms(x_ref[...]) * g_ref[...]
    o_ref[...] = (y * (1.0 + sc_ref[...]) + sh_ref[...]).astype(o_ref.dtype)


def _norm_mod(x, g, scale, shift):
    t, d = x.shape
    tm = min(512, t)
    row = pl.BlockSpec((1, d), lambda i: (0, 0))
    return pl.pallas_call(
        _norm_mod_kernel,
        grid=(t // tm,),
        in_specs=[pl.BlockSpec((tm, d), lambda i: (i, 0)), row, row, row],
        out_specs=pl.BlockSpec((tm, d), lambda i: (i, 0)),
        out_shape=jax.ShapeDtypeStruct((t, d), BF16),
        compiler_params=_params(1),
        name="norm_mod",
    )(x, g, scale, shift)


def _final_norm_kernel(x_ref, g_ref, o_ref):
    o_ref[...] = _rms(x_ref[...]) * g_ref[...]


def _final_norm(x, g):
    t, d = x.shape
    tm = min(512, t)
    return pl.pallas_call(
        _final_norm_kernel,
        grid=(t // tm,),
        in_specs=[pl.BlockSpec((tm, d), lambda i: (i, 0)), pl.BlockSpec((1, d), lambda i: (0, 0))],
        out_specs=pl.BlockSpec((tm, d), lambda i: (i, 0)),
        out_shape=jax.ShapeDtypeStruct((t, d), F32),
        compiler_params=_params(1),
        name="final_norm",
    )(x, g)


def _rope_table_kernel(pos_ref, f_ref, c_ref, sa_ref, sb_ref):
    ang = pos_ref[...] * f_ref[...]
    cos, sin = jnp.cos(ang), jnp.sin(ang)
    lane = lax.broadcasted_iota(jnp.int32, ang.shape, 1)
    c_ref[...] = jnp.where(lane < ROT_DIM, cos, 1.0)
    sa_ref[...] = jnp.where(lane < ROT_HALF, -sin, 0.0)
    sb_ref[...] = jnp.where((lane >= ROT_HALF) & (lane < ROT_DIM), sin, 0.0)


def _rope_tables(positions):
    t = positions.shape[-1]
    tm = min(1024, t)
    inv_freq = ROPE_THETA ** (-jnp.arange(0, ROT_DIM, 2, dtype=F32) / ROT_DIM)
    f_row = jnp.concatenate([inv_freq, inv_freq, jnp.zeros((LANES - ROT_DIM,), F32)])[None, :]
    pos = positions.astype(F32).reshape(t, 1)
    tab = jax.ShapeDtypeStruct((t, LANES), F32)
    blk = pl.BlockSpec((tm, LANES), lambda i: (i, 0))
    return pl.pallas_call(
        _rope_table_kernel,
        grid=(t // tm,),
        in_specs=[pl.BlockSpec((tm, 1), lambda i: (i, 0)), pl.BlockSpec((1, LANES), lambda i: (0, 0))],
        out_specs=[blk, blk, blk],
        out_shape=[tab, tab, tab],
        compiler_params=_params(1),
        name="rope_tables",
    )(pos, f_row)


def _proj_sconv_kernel(h_ref, wb_ref, wc_ref, wx_ref, cw_ref, o_ref, wb_a, wb_b, wc_a, wc_b, wx_a, wx_b, ext_ref):
    tm = h_ref.shape[0]

    def compute(w16):
        wb16, wc16, wx16 = w16

        @pl.when(pl.program_id(1) == 0)
        def _():
            ext_ref[0:SUBLANES, :] = jnp.zeros((SUBLANES, ext_ref.shape[1]), F32)

        h = h_ref[...]
        p = _dot(h, wc16[...]) * _dot(h, wx16[...])
        ext_ref[SUBLANES:, :] = p
        conv = (cw_ref[0:1, :] * ext_ref[pl.ds(SUBLANES - 2, tm), :]
                + cw_ref[1:2, :] * ext_ref[pl.ds(SUBLANES - 1, tm), :]
                + cw_ref[2:3, :] * p)
        o_ref[...] = (_dot(h, wb16[...]) * conv).astype(o_ref.dtype)
        ext_ref[0:SUBLANES, :] = ext_ref[tm:tm + SUBLANES, :]

    _run_streamed(((wb_ref, wb_a, wb_b), (wc_ref, wc_a, wc_b), (wx_ref, wx_a, wx_b)), compute)


def _proj_sconv(h, w_in, layer, conv_w, col0):
    t, d = h.shape
    tm, tn = min(1024, t), MXU_COLS
    nb = SC_WIDTH // tn
    n_inner = t // tm
    c0 = col0 // tn

    def wspec(sec):
        return _stream_weight_spec(layer, d, n_inner, tn, nb, lambda j: c0 + sec * nb + j)

    return pl.pallas_call(
        _proj_sconv_kernel,
        grid=(nb + 1, n_inner),
        in_specs=[_row_spec(tm, d), wspec(0), wspec(1), wspec(2), _col_spec(SC_KERNEL, tn)],
        out_specs=_tile_spec(tm, tn),
        out_shape=jax.ShapeDtypeStruct((t, SC_WIDTH), BF16),
        scratch_shapes=_w16_scratch(d, tn) * 3 + [pltpu.VMEM((tm + SUBLANES, tn), F32)],
        compiler_params=_params(2),
        name="proj_sconv",
    )(h, w_in, w_in, w_in, conv_w)


def _proj_glu_kernel(h_ref, wa_ref, wg_ref, o_ref, wa_a, wa_b, wg_a, wg_b):
    def compute(w16):
        wa16, wg16 = w16
        h = h_ref[...]
        for c0 in range(0, o_ref.shape[1], MXU_COLS):
            cs = slice(c0, c0 + MXU_COLS)
            o_ref[:, cs] = _dot(h, wa16[:, cs]) * _sigmoid(_dot(h, wg16[:, cs]))

    _run_streamed(((wa_ref, wa_a, wa_b), (wg_ref, wg_a, wg_b)), compute)


def _proj_glu(h, w_in, layer, col0):
    t, d = h.shape
    tm, tn = min(1024, t), 512
    nb = CF_WIDTH // tn
    n_inner = t // tm
    c0 = col0 // tn
    return pl.pallas_call(
        _proj_glu_kernel,
        grid=(nb + 1, n_inner),
        in_specs=[_row_spec(tm, d),
                  _stream_weight_spec(layer, d, n_inner, tn, nb, lambda j: c0 + j),
                  _stream_weight_spec(layer, d, n_inner, tn, nb, lambda j: c0 + nb + j)],
        out_specs=_tile_spec(tm, tn),
        out_shape=jax.ShapeDtypeStruct((t, CF_WIDTH), F32),
        scratch_shapes=_w16_scratch(d, tn) * 2,
        compiler_params=_params(2),
        name="proj_glu",
    )(h, w_in, w_in)


def _proj_qkv_kernel(h_ref, w_ref, c_ref, sa_ref, sb_ref, o_ref, w_a, w_b, slab_ref, *, dilation):
    tm = h_ref.shape[0]
    tn = w_a.shape[1]
    half = MXU_COLS
    rows = tm // dilation

    def compute(w16, rope):
        c, sa, sb = c_ref[...], sa_ref[...], sb_ref[...]
        h = h_ref[...]
        for c0 in range(0, tn, half):
            acc = _dot(h, w16[:, c0:c0 + half])
            for s in range(half // LANES):
                y = acc[:, s * LANES:(s + 1) * LANES]
                if rope:
                    y = y * c + pltpu.roll(y, LANES - ROT_HALF, 1) * sa + pltpu.roll(y, ROT_HALF, 1) * sb
                lanes = slice(c0 + s * LANES, c0 + (s + 1) * LANES)
                if dilation == 1:
                    o_ref[0, :, lanes] = y.astype(o_ref.dtype)
                else:
                    slab = c0 // LANES + s
                    slab_ref[slab] = y
                    for r in range(dilation):
                        o_ref[r, :, lanes] = slab_ref[slab, pl.ds(r, rows, stride=dilation), :].astype(o_ref.dtype)

    jj = pl.program_id(0)
    for step, (use, fill, rope) in enumerate(((None, w_a, False), (w_a, w_b, True), (w_b, w_a, True),
                                              (w_a, w_b, False))):
        @pl.when(jj == step)
        def _(use=use, fill=fill, rope=rope):
            _stream_cast([(w_ref, fill)])
            if use is not None:
                compute(use, rope)


def _proj_qkv(h, w_in, layer, tables, col_q, group, dilation):
    t, d = h.shape
    tm, tn = min(1024, t), DW_GROUP_WIDTH
    n_inner = t // tm
    c0 = col_q // tn + group
    sec_stride = DW_WIDTH // tn
    tab = _row_spec(tm, LANES)
    return pl.pallas_call(
        functools.partial(_proj_qkv_kernel, dilation=dilation),
        grid=(3 + 1, n_inner),
        in_specs=[_row_spec(tm, d),
                  _stream_weight_spec(layer, d, n_inner, tn, 3, lambda j: c0 + j * sec_stride),
                  tab, tab, tab],
        out_specs=pl.BlockSpec((None, dilation, tm // dilation, tn),
                               lambda jj, i: (jnp.maximum(jj - 1, 0), 0, jnp.where(jj == 0, 0, i), 0)),
        out_shape=jax.ShapeDtypeStruct((3, dilation, t // dilation, tn), BF16),
        scratch_shapes=_w16_scratch(d, tn) + [pltpu.VMEM((tn // LANES, tm, LANES), F32)],
        compiler_params=_params(2),
        name=f"proj_qkv_d{dilation}",
    )(h, w_in, *tables)


def _activate(acc, act):
    if act == "sigmoid":
        return _sigmoid(acc)
    if act == "relu2":
        return jnp.square(jnp.maximum(acc, 0.0))
    raise ValueError(act)


def _proj_act_kernel(h_ref, w_ref, *rest, act, cast_side):
    if cast_side:
        side_ref, o_ref, side16_ref, w_a, w_b = rest
    else:
        o_ref, w_a, w_b = rest
    tn = o_ref.shape[1]
    half = MXU_COLS

    def compute(w16):
        h = h_ref[...]
        for c0 in range(0, tn, half):
            cs = slice(c0, c0 + half)
            o_ref[:, cs] = _activate(_dot(h, w16[0][:, cs]), act).astype(o_ref.dtype)
        if cast_side:
            side16_ref[...] = side_ref[...].astype(BF16)

    _run_streamed(((w_ref, w_a, w_b),), compute)


def _proj_act(h, w, layer, col0, width, act, name, side=None):
    t, d = h.shape
    tm = min(1024, t)
    tn = 1024 if width % 1024 == 0 and col0 % 1024 == 0 else 512
    nb = width // tn
    n_inner = t // tm
    c0 = col0 // tn
    in_specs = [_row_spec(tm, d), _stream_weight_spec(layer, d, n_inner, tn, nb, lambda j: c0 + j)]
    out_specs = [_tile_spec(tm, tn)]
    out_shape = [jax.ShapeDtypeStruct((t, width), BF16)]
    args = [h, w]
    if side is not None:
        _, side_rows, side_cols = side.shape
        blk = side_rows // (nb * n_inner)
        assert blk * nb * n_inner == side_rows and blk % 16 == 0

        def side_block(jj, i):
            return jnp.where(jj == 0, 0, (jj - 1) * n_inner + i)

        in_specs.append(pl.BlockSpec((None, blk, side_cols), lambda jj, i: (layer, side_block(jj, i), 0)))
        out_specs.append(pl.BlockSpec((blk, side_cols), lambda jj, i: (side_block(jj, i), 0)))
        out_shape.append(jax.ShapeDtypeStruct((side_rows, side_cols), BF16))
        args.append(side)
    outs = pl.pallas_call(
        functools.partial(_proj_act_kernel, act=act, cast_side=side is not None),
        grid=(nb + 1, n_inner),
        in_specs=in_specs,
        out_specs=out_specs,
        out_shape=out_shape,
        scratch_shapes=_w16_scratch(d, tn),
        compiler_params=_params(2),
        name=name,
    )(*args)
    return outs if side is not None else outs[0]


CF_ROWS = 64
CF_COLS = 256


def _conv_prepare(is_first, uh_ref, u_ref, sh_ref):
    n_ext = u_ref.shape[0] + CF_HALO
    sh_ref[0, 0:CF_HALO, :] = jnp.where(is_first, 0.0, uh_ref[...])
    sh_ref[0, CF_HALO:, :] = u_ref[...]
    for s in range(1, SUBLANES):
        sh_ref[s, 0:n_ext - SUBLANES, :] = sh_ref[0, pl.ds(s, n_ext - SUBLANES), :]


def _conv_columns(w_ref, b_ref, sh_ref, y_ref, col_starts):
    first = CF_HALO - (CF_KERNEL - 1)
    for c0 in col_starts:
        cs = slice(c0, c0 + CF_COLS)
        for r0 in range(0, y_ref.shape[0], CF_ROWS):
            acc = jnp.broadcast_to(b_ref[:, cs], (CF_ROWS, CF_COLS))
            for k in range(CF_KERNEL):
                shift = (first + k) % SUBLANES
                base = r0 + first + k - shift
                acc = acc + w_ref[k:k + 1, cs] * sh_ref[shift, base:base + CF_ROWS, cs]
            y_ref[r0:r0 + CF_ROWS, cs] = acc


def _conv_finish(y_ref, g_ref, be_ref, o_ref):
    y = y_ref[...]
    mu = jnp.mean(y, axis=-1, keepdims=True)
    yc = y - mu
    var = jnp.mean(yc * yc, axis=-1, keepdims=True)
    z = yc * lax.rsqrt(var + EPS) * g_ref[...] + be_ref[...]
    o_ref[...] = (z * _sigmoid(z)).astype(o_ref.dtype)


def _gates_conformer_kernel(h_ref, w_ref, uh_ref, u_ref, cw_ref, cb_ref, lg_ref, lb_ref, *rest,
                            conf_steps, next_mod):
    rest = list(rest)
    c_ref, wm_ref, bm_ref = (rest.pop(0), rest.pop(0), rest.pop(0)) if next_mod else (None, None, None)
    o_ref, zb_ref = rest.pop(0), rest.pop(0)
    mod_ref = rest.pop(0) if next_mod else None
    w_a, w_b, sh_ref, y_ref = rest
    jj = pl.program_id(0)
    tm, tn = o_ref.shape
    dots = [(slice(r0, r0 + tm // 2), slice(c0, c0 + MXU_COLS))
            for c0 in range(0, tn, MXU_COLS) for r0 in (0, tm // 2)]
    conv_cols = list(range(0, u_ref.shape[1], CF_COLS))
    per_dot = -(-len(conv_cols) // len(dots))

    def body(use, fill, with_conf):
        _stream_cast([(w_ref, fill)])
        if with_conf:
            _conv_prepare((jj == 1) & (pl.program_id(1) == 0), uh_ref, u_ref, sh_ref)
        for idx, (rs, cs) in enumerate(dots):
            if with_conf:
                _conv_columns(cw_ref, cb_ref, sh_ref, y_ref, conv_cols[idx * per_dot:(idx + 1) * per_dot])
            if next_mod and idx == 0:
                mod_ref[...] = _ada_block(c_ref, wm_ref, bm_ref)
            o_ref[rs, cs] = _sigmoid(_dot(h_ref[rs, :], use[:, cs])).astype(o_ref.dtype)
        if with_conf:
            _conv_finish(y_ref, lg_ref, lb_ref, zb_ref)

    @pl.when(jj == 0)
    def _():
        _stream_cast([(w_ref, w_a)])

    odd = jj % 2 == 1
    conf = jj <= conf_steps
    for parity, use, fill in ((True, w_a, w_b), (False, w_b, w_a)):
        for with_conf in (True, False):
            @pl.when((jj > 0) & (odd == parity) & (conf == with_conf))
            def _(use=use, fill=fill, with_conf=with_conf):
                body(use, fill, with_conf)


def _gates_conformer(h, w_in, layer, col0, width, u, conv_w, bias, ln_g, ln_b, ada=None):
    t, d = h.shape
    cw = u.shape[1]
    tm, tn = min(1024, t), 512
    nb = width // tn
    n_inner = t // tm
    n_steps = nb * n_inner
    c0 = col0 // tn
    conf_steps = nb
    while t % (conf_steps * n_inner * CF_ROWS):
        conf_steps -= 1
    cf_tm = t // (conf_steps * n_inner)
    per = cf_tm // CF_HALO
    last = conf_steps * n_inner - 1

    def tile(jj, i):
        return jnp.clip((jj - 1) * n_inner + i, 0, last)

    def step(jj, i):
        return jnp.where(jj == 0, 0, (jj - 1) * n_inner + i)

    row = pl.BlockSpec((1, cw), lambda jj, i: (0, 0))
    in_specs = [_row_spec(tm, d), _stream_weight_spec(layer, d, n_inner, tn, nb, lambda j: c0 + j),
                pl.BlockSpec((CF_HALO, cw), lambda jj, i: (jnp.maximum(tile(jj, i) * per - 1, 0), 0)),
                pl.BlockSpec((cf_tm, cw), lambda jj, i: (tile(jj, i), 0)),
                pl.BlockSpec((CF_KERNEL, cw), lambda jj, i: (0, 0)), row, row, row]
    out_specs = [_tile_spec(tm, tn), pl.BlockSpec((cf_tm, cw), lambda jj, i: (tile(jj, i), 0))]
    out_shape = [jax.ShapeDtypeStruct((t, width), BF16), jax.ShapeDtypeStruct((t, cw), BF16)]
    args = [h, w_in, u, u, conv_w, bias, ln_g, ln_b]
    if ada is not None:
        c, w_ada, b_ada = ada
        _, dm, n = w_ada.shape
        cols = n // n_steps
        assert cols * n_steps == n and cols % LANES == 0
        in_specs += [pl.BlockSpec((dm, LANES), lambda jj, i: (0, 0)),
                     pl.BlockSpec((None, dm, cols), lambda jj, i: (layer + 1, 0, step(jj, i))),
                     pl.BlockSpec((None, 1, cols), lambda jj, i: (layer + 1, 0, step(jj, i)))]
        out_specs.append(pl.BlockSpec((1, cols), lambda jj, i: (0, step(jj, i))))
        out_shape.append(jax.ShapeDtypeStruct((1, n), F32))
        args += [_lane_broadcast(c), w_ada, b_ada.reshape(-1, 1, n)]
    return pl.pallas_call(
        functools.partial(_gates_conformer_kernel, conf_steps=conf_steps, next_mod=ada is not None),
        grid=(nb + 1, n_inner),
        in_specs=in_specs,
        out_specs=out_specs,
        out_shape=out_shape,
        scratch_shapes=_w16_scratch(d, tn) + [pltpu.VMEM((SUBLANES, cf_tm + CF_HALO, cw), F32),
                                              pltpu.VMEM((cf_tm, cw), F32)],
        compiler_params=_params(2),
        name="gates_conformer",
    )(*args)


def _attn_kernel(q_ref, kp_ref, kc_ref, vp_ref, vc_ref, o_ref, l_ref):
    n = ATTN_BLOCK
    n_blocks = q_ref.shape[0] // n
    k = jnp.concatenate([kp_ref[...], kc_ref[...]], axis=0)
    v = jnp.concatenate([vp_ref[...], vc_ref[...]], axis=0)
    qi = lax.broadcasted_iota(jnp.int32, (n, 2 * n), 0)
    kj = lax.broadcasted_iota(jnp.int32, (n, 2 * n), 1)
    band = (kj >= qi) & (kj <= qi + n)
    has_prev = pl.program_id(1) > 0
    l_ref[...] = jnp.zeros(l_ref.shape, F32)
    for jb in range(n_blocks):
        rows = slice(jb * n, (jb + 1) * n)
        keys = slice(jb * n, (jb + 2) * n)
        valid = band & ((kj >= n) | has_prev) if jb == 0 else band
        for h in range(DW_HEADS):
            sl = slice(h * HEAD_DIM, (h + 1) * HEAD_DIM)
            s = lax.dot_general(q_ref[rows, sl], k[keys, sl], (((1,), (1,)), ((), ())),
                                preferred_element_type=F32) * (HEAD_DIM ** -0.5)
            s = jnp.where(valid, s, -jnp.inf)
            m = jnp.max(s, axis=-1, keepdims=True)
            p = jnp.exp(s - m)
            den = jnp.sum(p, axis=-1, keepdims=True)
            o_ref[rows, sl] = _dot((p / den).astype(BF16), v[keys, sl])
            l_ref[rows, h:h + 1] = m + jnp.log(den)


def _attention_group(qkv):
    _, dilation, sub, gw = qkv.shape
    n = ATTN_BLOCK
    assert sub % n == 0
    per_step = min(4, sub // n)
    tq = per_step * n
    assert sub % tq == 0

    def cur(sec):
        return pl.BlockSpec((None, None, tq, gw), lambda r, b: (sec, r, b, 0))

    def prev(sec):
        return pl.BlockSpec((None, None, n, gw), lambda r, b: (sec, r, jnp.maximum(b * per_step - 1, 0), 0))

    return pl.pallas_call(
        _attn_kernel,
        grid=(dilation, sub // tq),
        in_specs=[cur(0), prev(1), cur(1), prev(2), cur(2)],
        out_specs=[pl.BlockSpec((None, tq, gw), lambda r, b: (r, b, 0)),
                   pl.BlockSpec((None, tq, LANES), lambda r, b: (r, b, 0))],
        out_shape=[jax.ShapeDtypeStruct((dilation, sub, gw), F32),
                   jax.ShapeDtypeStruct((dilation, sub, LANES), F32)],
        compiler_params=_params(2),
        name=f"dilated_attn_d{dilation}",
    )(qkv, qkv, qkv, qkv, qkv)


def _combine_kernel(o0_ref, o1_ref, o2_ref, l0_ref, l1_ref, l2_ref, out_ref, lnat_ref, onat_ref):
    o_refs = (o0_ref, o1_ref, o2_ref)
    l_refs = (l0_ref, l1_ref, l2_ref)
    tm = out_ref.shape[0]

    def natural(dst_ref, src, dilation):
        rows = tm // dilation
        for r in range(dilation):
            dst_ref[pl.ds(r, rows, stride=dilation), :] = src(r)

    lse = []
    for g, l_ref in enumerate(l_refs):
        dilation = l_ref.shape[0]
        if dilation == 1:
            lse.append(l_ref[0])
        else:
            natural(lnat_ref.at[g], lambda r, l_ref=l_ref: l_ref[r], dilation)
            lse.append(lnat_ref[g])
    m = jnp.maximum(jnp.maximum(lse[0], lse[1]), lse[2])
    e = [jnp.exp(l - m) for l in lse]
    den = e[0] + e[1] + e[2]
    w = [x / den for x in e]
    for h in range(DW_HEADS):
        sl = slice(h * HEAD_DIM, (h + 1) * HEAD_DIM)
        acc = None
        for g, o_ref in enumerate(o_refs):
            dilation = o_ref.shape[0]
            if dilation == 1:
                o_nat = o_ref[0, :, sl]
            else:
                natural(onat_ref, lambda r, o_ref=o_ref: o_ref[r, :, sl], dilation)
                o_nat = onat_ref[...]
            term = w[g][:, h:h + 1] * o_nat
            acc = term if acc is None else acc + term
        out_ref[:, sl] = acc.astype(out_ref.dtype)


def _combine(outs, lses):
    gw = outs[0].shape[-1]
    t = outs[0].shape[0] * outs[0].shape[1]
    tm = min(512, t)

    def spec(a):
        dil = a.shape[0]
        return pl.BlockSpec((dil, tm // dil, a.shape[-1]), lambda i: (0, i, 0))

    return pl.pallas_call(
        _combine_kernel,
        grid=(t // tm,),
        in_specs=[spec(a) for a in outs] + [spec(a) for a in lses],
        out_specs=pl.BlockSpec((tm, gw), lambda i: (i, 0)),
        out_shape=jax.ShapeDtypeStruct((t, gw), BF16),
        scratch_shapes=[pltpu.VMEM((N_DW_GROUPS, tm, LANES), F32), pltpu.VMEM((tm, LANES), F32)],
        compiler_params=_params(1),
        name="attn_combine",
    )(*outs, *lses)


def _merge_kernel(za_ref, zb_ref, zc_ref, wa_ref, wb_ref, wc_ref, ga_ref, gb_ref, gc_ref, o_ref,
                  wa_a, wa_b, wb_a, wb_b, wc_a, wc_b):
    def compute(w16):
        wa16, wb16, wc16 = w16
        m = ga_ref[...].astype(F32) * _dot(za_ref[...], wa16[...])
        m = m + gb_ref[...].astype(F32) * _dot(zb_ref[...], wb16[...])
        m = m + gc_ref[...].astype(F32) * _dot(zc_ref[...], wc16[...])
        o_ref[...] = m.astype(o_ref.dtype)

    _run_streamed(((wa_ref, wa_a, wa_b), (wb_ref, wb_a, wb_b), (wc_ref, wc_a, wc_b)), compute)


def _merge(za, zb, zc, wa, wb, wc, layer, gates):
    t = za.shape[0]
    d = wa.shape[-1]
    tm, tn = min(1024, t), min(512, d)
    nb = d // tn
    n_inner = t // tm

    def wgt(a):
        return _stream_weight_spec(layer, a.shape[1], n_inner, tn, nb, lambda j: j)

    def gate(sec):
        return _tile_spec(tm, tn, lambda j: sec * nb + j)

    return pl.pallas_call(
        _merge_kernel,
        grid=(nb + 1, n_inner),
        in_specs=[_row_spec(tm, za.shape[1]), _row_spec(tm, zb.shape[1]), _row_spec(tm, zc.shape[1]),
                  wgt(wa), wgt(wb), wgt(wc), gate(0), gate(1), gate(2)],
        out_specs=_tile_spec(tm, tn),
        out_shape=jax.ShapeDtypeStruct((t, d), BF16),
        scratch_shapes=[buf for a in (wa, wb, wc) for buf in _w16_scratch(a.shape[1], tn)],
        compiler_params=_params(2),
        name="merge_branches",
    )(za, zb, zc, wa, wb, wc, gates, gates, gates)


def _out_proj_kernel(a_ref, w_ref, x_ref, g_ref, o_ref, w_a, w_b):
    def compute(w16):
        o_ref[...] = x_ref[...] + g_ref[...] * _dot(a_ref[...], w16[0][...])

    _run_streamed(((w_ref, w_a, w_b),), compute)


def _out_proj(a, w, layer, x, gate):
    t, kdim = a.shape
    d = w.shape[-1]
    tm, tn = min(1024, t), min(1024, d)
    nb = d // tn
    n_inner = t // tm
    return pl.pallas_call(
        _out_proj_kernel,
        grid=(nb + 1, n_inner),
        in_specs=[_row_spec(tm, kdim), _stream_weight_spec(layer, kdim, n_inner, tn, nb, lambda j: j),
                  _tile_spec(tm, tn), _col_spec(1, tn)],
        out_specs=_tile_spec(tm, tn),
        out_shape=jax.ShapeDtypeStruct((t, d), F32),
        scratch_shapes=_w16_scratch(kdim, tn),
        compiler_params=_params(2),
        name="out_proj_resid",
    )(a, w, x, gate)


def _resid_kernel(a_ref, w_ref, x_ref, g_ref, o_ref, acc_ref):
    kk = pl.program_id(2)
    last = pl.num_programs(2) - 1

    @pl.when(kk == 0)
    def _():
        acc_ref[...] = _dot(a_ref[...], w_ref[...])

    @pl.when((kk > 0) & (kk < last))
    def _():
        acc_ref[...] += _dot(a_ref[...], w_ref[...])

    @pl.when(kk == last)
    def _():
        o_ref[...] = x_ref[...] + g_ref[...] * (acc_ref[...] + _dot(a_ref[...], w_ref[...]))


def _resid_matmul(a, w, x, gate, name):
    t, kdim = a.shape
    d = w.shape[1]
    tm, tn, tk = min(1024, t), min(1024, d), min(4096, kdim // 2)
    return pl.pallas_call(
        _resid_kernel,
        grid=(d // tn, t // tm, kdim // tk),
        in_specs=[pl.BlockSpec((tm, tk), lambda j, i, k: (i, k)),
                  pl.BlockSpec((tk, tn), lambda j, i, k: (k, j)),
                  pl.BlockSpec((tm, tn), lambda j, i, k: (i, j)),
                  pl.BlockSpec((1, tn), lambda j, i, k: (0, j))],
        out_specs=pl.BlockSpec((tm, tn), lambda j, i, k: (i, j)),
        out_shape=jax.ShapeDtypeStruct((t, d), F32),
        scratch_shapes=[pltpu.VMEM((tm, tn), F32)],
        compiler_params=_params(3),
        name=name,
    )(a, w, x, gate)


def kernel(x, c, positions, w_ada, b_ada, g_mix, w_in, conv_a, conv_b, conv_b_bias, ln_cf_g, ln_cf_b,
           w_out_a, w_out_b, w_out_c, w_o, g_mlp, w_mlp1, w_mlp2, g_final):
    b, t, d = x.shape
    assert b == 1, "kernel written for a single sequence"
    depth = w_ada.shape[0]
    d_ff = w_mlp1.shape[-1]
    xs = x[0]

    mod = _ada(c, w_ada, b_ada, 0)
    tables = _rope_tables(positions[0])

    col_sc = 0
    col_cf = col_sc + 3 * SC_WIDTH
    col_q = col_cf + 2 * CF_WIDTH
    col_gate = col_q + 3 * DW_WIDTH

    for l in range(depth):
        shift1, scale1, gate1, shift2, scale2, gate2 = (mod[:, i * d:(i + 1) * d] for i in range(N_MOD))

        h = _norm_mod(xs, g_mix[l][None, :], scale1, shift1)
        z_a = _proj_sconv(h, w_in, l, conv_a[l], col_sc)
        u = _proj_glu(h, w_in, l, col_cf)

        outs, lses = [], []
        for g, (window, dilation) in enumerate(DW_PATTERNS):
            assert window // dilation == ATTN_BLOCK
            o_g, lse_g = _attention_group(_proj_qkv(h, w_in, l, tables, col_q, g, dilation))
            outs.append(o_g)
            lses.append(lse_g)
        z_c = _combine(outs, lses)

        conf_args = (u, conv_b[l], conv_b_bias[l][None, :], ln_cf_g[l][None, :], ln_cf_b[l][None, :])
        if l + 1 < depth:
            gates, z_b, next_mod = _gates_conformer(h, w_in, l, col_gate, 3 * d, *conf_args,
                                                    ada=(c, w_ada, b_ada))
        else:
            gates, z_b = _gates_conformer(h, w_in, l, col_gate, 3 * d, *conf_args)
        merged = _merge(z_a, z_b, z_c, w_out_a, w_out_b, w_out_c, l, gates)
        xs = _out_proj(merged, w_o, l, xs, gate1)

        h2 = _norm_mod(xs, g_mlp[l][None, :], scale2, shift2)
        a, w_down = _proj_act(h2, w_mlp1, l, 0, d_ff, "relu2", "mlp_up", side=w_mlp2)
        xs = _resid_matmul(a, w_down, xs, gate2, "mlp_down_resid")
        if l + 1 < depth:
            mod = next_mod

    return _final_norm(xs, g_final[None, :])[None]
```

```python
import functools

import jax
import jax.numpy as jnp
from jax import lax
from jax.experimental import pallas as pl
from jax.experimental.pallas import tpu as pltpu

F32 = jnp.float32
BF16 = jnp.bfloat16

EPS = 1e-6
SC_WIDTH = 2048
SC_KERNEL = 3
CF_WIDTH = 2048
CF_KERNEL = 31
HEAD_DIM = 128
DW_PATTERNS = ((128, 1), (512, 4), (2048, 16))
N_DW_GROUPS = len(DW_PATTERNS)
DW_HEADS = 8
DW_GROUP_WIDTH = DW_HEADS * HEAD_DIM
DW_WIDTH = N_DW_GROUPS * DW_GROUP_WIDTH
ATTN_BLOCK = 128
ROT_DIM = HEAD_DIM // 4
ROT_HALF = ROT_DIM // 2
ROPE_THETA = 500000.0
N_MOD = 6

LANES = 128
SUBLANES = 8
MXU_COLS = 256
CF_HALO = 32
VMEM_LIMIT = 61 * 1024 * 1024


def _params(n_axes):
    return pltpu.CompilerParams(dimension_semantics=("arbitrary",) * n_axes,
                                vmem_limit_bytes=VMEM_LIMIT)


def _dot(a, b):
    return jnp.dot(a, b, preferred_element_type=F32)


def _sigmoid(v):
    return jax.nn.sigmoid(v)


def _stream_weight_spec(layer, k_rows, n_chunks, tn, n_col_tiles, col_block):
    chunk = k_rows // n_chunks
    assert chunk * n_chunks == k_rows and chunk % 16 == 0

    def imap(jj, i):
        return (layer, jnp.where(jj < n_col_tiles, i, 0), col_block(jnp.minimum(jj, n_col_tiles - 1)))

    return pl.BlockSpec((None, chunk, tn), imap)


def _row_spec(tm, width):
    return pl.BlockSpec((tm, width), lambda jj, i: (jnp.where(jj == 0, 0, i), 0))


def _tile_spec(tm, tn, col_block=lambda j: j):
    return pl.BlockSpec((tm, tn), lambda jj, i: (jnp.where(jj == 0, 0, i), col_block(jnp.maximum(jj - 1, 0))))


def _col_spec(rows, tn):
    return pl.BlockSpec((rows, tn), lambda jj, i: (0, jnp.maximum(jj - 1, 0)))


def _stream_cast(pairs):
    for w_ref, w16 in pairs:
        rows = w_ref.shape[0]
        r0 = pl.multiple_of(pl.program_id(1) * rows, rows)
        w16[pl.ds(r0, rows), :] = w_ref[...].astype(BF16)


def _run_streamed(streams, compute):
    jj = pl.program_id(0)
    fill_a = [(w, a) for w, a, _ in streams]
    fill_b = [(w, b) for w, _, b in streams]

    @pl.when(jj == 0)
    def _():
        _stream_cast(fill_a)

    @pl.when(jj % 2 == 1)
    def _():
        _stream_cast(fill_b)
        compute([a for _, a, _ in streams])

    @pl.when((jj > 0) & (jj % 2 == 0))
    def _():
        _stream_cast(fill_a)
        compute([b for _, _, b in streams])


def _w16_scratch(k_rows, tn):
    return [pltpu.VMEM((k_rows, tn), BF16)] * 2


def _ada_block(cb_ref, w_ref, b_ref):
    c_act = _silu_ref(cb_ref)
    parts = [_ada_slab(c_act, w_ref, b_ref, j) for j in range(0, w_ref.shape[1], LANES)]
    return jnp.concatenate(parts, axis=1)


def _silu_ref(cb_ref):
    cb = cb_ref[...]
    return cb * _sigmoid(cb)


def _ada_slab(c_act, w_ref, b_ref, j):
    lanes = slice(j, j + LANES)
    return jnp.sum(w_ref[:, lanes] * c_act, axis=0, keepdims=True) + b_ref[:, lanes]


def _ada_kernel(cb_ref, w_ref, b_ref, o_ref):
    o_ref[...] = _ada_block(cb_ref, w_ref, b_ref)


def _lane_broadcast(c):
    return jnp.broadcast_to(c.reshape(-1, 1), (c.size, LANES))


def _ada(c, w_ada, b_ada, layer):
    _, d, n = w_ada.shape
    tn = min(512, n)
    return pl.pallas_call(
        _ada_kernel,
        grid=(n // tn,),
        in_specs=[pl.BlockSpec((d, LANES), lambda j: (0, 0)),
                  pl.BlockSpec((None, d, tn), lambda j: (layer, 0, j)),
                  pl.BlockSpec((None, 1, tn), lambda j: (layer, 0, j))],
        out_specs=pl.BlockSpec((1, tn), lambda j: (0, j)),
        out_shape=jax.ShapeDtypeStruct((1, n), F32),
        compiler_params=_params(1),
        name="ada_mod",
    )(_lane_broadcast(c), w_ada, b_ada.reshape(-1, 1, n))


def _rms(x):
    return x * lax.rsqrt(jnp.mean(x * x, axis=-1, keepdims=True) + EPS)


def _norm_mod_kernel(x_ref, g_ref, sc_ref, sh_ref, o_ref):
    y = _rms(x_ref[...]) * g_ref[...]
    o_ref[...] = (y * (1.0 + sc_ref[...]) + sh_ref[...]).astype(o_ref.dtype)


def _norm_mod(x, g, scale, shift):
    t, d = x.shape
    tm = min(512, t)
    row = pl.BlockSpec((1, d), lambda i: (0, 0))
    return pl.pallas_call(
        _norm_mod_kernel,
        grid=(t // tm,),
        in_specs=[pl.BlockSpec((tm, d), lambda i: (i, 0)), row, row, row],
        out_specs=pl.BlockSpec((tm, d), lambda i: (i, 0)),
        out_shape=jax.ShapeDtypeStruct((t, d), BF16),
        compiler_params=_params(1),
        name="norm_mod",
    )(x, g, scale, shift)


def _final_norm_kernel(x_ref, g_ref, o_ref):
    o_ref[...] = _rms(x_ref[...]) * g_ref[...]


def _final_norm(x, g):
    t, d = x.shape
    tm = min(512, t)
    return pl.pallas_call(
        _final_norm_kernel,
        grid=(t // tm,),
        in_specs=[pl.BlockSpec((tm, d), lambda i: (i, 0)), pl.BlockSpec((1, d), lambda i: (0, 0))],
        out_specs=pl.BlockSpec((tm, d), lambda i: (i, 0)),
        out_shape=jax.ShapeDtypeStruct((t, d), F32),
        compiler_params=_params(1),
        name="final_norm",
    )(x, g)


def _rope_table_kernel(pos_ref, f_ref, c_ref, sa_ref, sb_ref):
    ang = pos_ref[...] * f_ref[...]
    cos, sin = jnp.cos(ang), jnp.sin(ang)
    lane = lax.broadcasted_iota(jnp.int32, ang.shape, 1)
    c_ref[...] = jnp.where(lane < ROT_DIM, cos, 1.0)
    sa_ref[...] = jnp.where(lane < ROT_HALF, -sin, 0.0)
    sb_ref[...] = jnp.where((lane >= ROT_HALF) & (lane < ROT_DIM), sin, 0.0)


def _rope_tables(positions):
    t = positions.shape[-1]
    tm = min(1024, t)
    inv_freq = ROPE_THETA ** (-jnp.arange(0, ROT_DIM, 2, dtype=F32) / ROT_DIM)
    f_row = jnp.concatenate([inv_freq, inv_freq, jnp.zeros((LANES - ROT_DIM,), F32)])[None, :]
    pos = positions.astype(F32).reshape(t, 1)
    tab = jax.ShapeDtypeStruct((t, LANES), F32)
    blk = pl.BlockSpec((tm, LANES), lambda i: (i, 0))
    return pl.pallas_call(
        _rope_table_kernel,
        grid=(t // tm,),
        in_specs=[pl.BlockSpec((tm, 1), lambda i: (i, 0)), pl.BlockSpec((1, LANES), lambda i: (0, 0))],
        out_specs=[blk, blk, blk],
        out_shape=[tab, tab, tab],
        compiler_params=_params(1),
        name="rope_tables",
    )(pos, f_row)


def _proj_sconv_kernel(h_ref, wb_ref, wc_ref, wx_ref, cw_ref, o_ref, wb_a, wb_b, wc_a, wc_b, wx_a, wx_b, ext_ref):
    tm = h_ref.shape[0]

    def compute(w16):
        wb16, wc16, wx16 = w16

        @pl.when(pl.program_id(1) == 0)
        def _():
            ext_ref[0:SUBLANES, :] = jnp.zeros((SUBLANES, ext_ref.shape[1]), F32)

        h = h_ref[...]
        p = _dot(h, wc16[...]) * _dot(h, wx16[...])
        ext_ref[SUBLANES:, :] = p
        conv = (cw_ref[0:1, :] * ext_ref[pl.ds(SUBLANES - 2, tm), :]
                + cw_ref[1:2, :] * ext_ref[pl.ds(SUBLANES - 1, tm), :]
                + cw_ref[2:3, :] * p)
        o_ref[...] = (_dot(h, wb16[...]) * conv).astype(o_ref.dtype)
        ext_ref[0:SUBLANES, :] = ext_ref[tm:tm + SUBLANES, :]

    _run_streamed(((wb_ref, wb_a, wb_b), (wc_ref, wc_a, wc_b), (wx_ref, wx_a, wx_b)), compute)


def _proj_sconv(h, w_in, layer, conv_w, col0):
    t, d = h.shape
    tm, tn = min(1024, t), MXU_COLS
    nb = SC_WIDTH // tn
    n_inner = t // tm
    c0 = col0 // tn

    def wspec(sec):
        return _stream_weight_spec(layer, d, n_inner, tn, nb, lambda j: c0 + sec * nb + j)

    return pl.pallas_call(
        _proj_sconv_kernel,
        grid=(nb + 1, n_inner),
        in_specs=[_row_spec(tm, d), wspec(0), wspec(1), wspec(2), _col_spec(SC_KERNEL, tn)],
        out_specs=_tile_spec(tm, tn),
        out_shape=jax.ShapeDtypeStruct((t, SC_WIDTH), BF16),
        scratch_shapes=_w16_scratch(d, tn) * 3 + [pltpu.VMEM((tm + SUBLANES, tn), F32)],
        compiler_params=_params(2),
        name="proj_sconv",
    )(h, w_in, w_in, w_in, conv_w)


def _proj_glu_kernel(h_ref, wa_ref, wg_ref, o_ref, wa_a, wa_b, wg_a, wg_b):
    def compute(w16):
        wa16, wg16 = w16
        h = h_ref[...]
        for c0 in range(0, o_ref.shape[1], MXU_COLS):
            cs = slice(c0, c0 + MXU_COLS)
            o_ref[:, cs] = _dot(h, wa16[:, cs]) * _sigmoid(_dot(h, wg16[:, cs]))

    _run_streamed(((wa_ref, wa_a, wa_b), (wg_ref, wg_a, wg_b)), compute)


def _proj_glu(h, w_in, layer, col0):
    t, d = h.shape
    tm, tn = min(1024, t), 512
    nb = CF_WIDTH // tn
    n_inner = t // tm
    c0 = col0 // tn
    return pl.pallas_call(
        _proj_glu_kernel,
        grid=(nb + 1, n_inner),
        in_specs=[_row_spec(tm, d),
                  _stream_weight_spec(layer, d, n_inner, tn, nb, lambda j: c0 + j),
                  _stream_weight_spec(layer, d, n_inner, tn, nb, lambda j: c0 + nb + j)],
        out_specs=_tile_spec(tm, tn),
        out_shape=jax.ShapeDtypeStruct((t, CF_WIDTH), F32),
        scratch_shapes=_w16_scratch(d, tn) * 2,
        compiler_params=_params(2),
        name="proj_glu",
    )(h, w_in, w_in)


def _proj_qkv_kernel(h_ref, w_ref, c_ref, sa_ref, sb_ref, o_ref, w_a, w_b, slab_ref, *, dilation):
    tm = h_ref.shape[0]
    tn = w_a.shape[1]
    half = MXU_COLS
    rows = tm // dilation

    def compute(w16, rope):
        c, sa, sb = c_ref[...], sa_ref[...], sb_ref[...]
        h = h_ref[...]
        for c0 in range(0, tn, half):
            acc = _dot(h, w16[:, c0:c0 + half])
            for s in range(half // LANES):
                y = acc[:, s * LANES:(s + 1) * LANES]
                if rope:
                    y = y * c + pltpu.roll(y, LANES - ROT_HALF, 1) * sa + pltpu.roll(y, ROT_HALF, 1) * sb
                lanes = slice(c0 + s * LANES, c0 + (s + 1) * LANES)
                if dilation == 1:
                    o_ref[0, :, lanes] = y.astype(o_ref.dtype)
                else:
                    slab = c0 // LANES + s
                    slab_ref[slab] = y
                    for r in range(dilation):
                        o_ref[r, :, lanes] = slab_ref[slab, pl.ds(r, rows, stride=dilation), :].astype(o_ref.dtype)

    jj = pl.program_id(0)
    for step, (use, fill, rope) in enumerate(((None, w_a, False), (w_a, w_b, True), (w_b, w_a, True),
                                              (w_a, w_b, False))):
        @pl.when(jj == step)
        def _(use=use, fill=fill, rope=rope):
            _stream_cast([(w_ref, fill)])
            if use is not None:
                compute(use, rope)


def _proj_qkv(h, w_in, layer, tables, col_q, group, dilation):
    t, d = h.shape
    tm, tn = min(1024, t), DW_GROUP_WIDTH
    n_inner = t // tm
    c0 = col_q // tn + group
    sec_stride = DW_WIDTH // tn
    tab = _row_spec(tm, LANES)
    return pl.pallas_call(
        functools.partial(_proj_qkv_kernel, dilation=dilation),
        grid=(3 + 1, n_inner),
        in_specs=[_row_spec(tm, d),
                  _stream_weight_spec(layer, d, n_inner, tn, 3, lambda j: c0 + j * sec_stride),
                  tab, tab, tab],
        out_specs=pl.BlockSpec((None, dilation, tm // dilation, tn),
                               lambda jj, i: (jnp.maximum(jj - 1, 0), 0, jnp.where(jj == 0, 0, i), 0)),
        out_shape=jax.ShapeDtypeStruct((3, dilation, t // dilation, tn), BF16),
        scratch_shapes=_w16_scratch(d, tn) + [pltpu.VMEM((tn // LANES, tm, LANES), F32)],
        compiler_params=_params(2),
        name=f"proj_qkv_d{dilation}",
    )(h, w_in, *tables)


def _activate(acc, act):
    if act == "sigmoid":
        return _sigmoid(acc)
    if act == "relu2":
        return jnp.square(jnp.maximum(acc, 0.0))
    raise ValueError(act)


def _proj_act_kernel(h_ref, w_ref, *rest, act, cast_side, next_mod):
    rest = list(rest)
    side_ref = rest.pop(0) if cast_side else None
    cb_ref, wm_ref, bm_ref = (rest.pop(0), rest.pop(0), rest.pop(0)) if next_mod else (None, None, None)
    o_ref = rest.pop(0)
    side16_ref = rest.pop(0) if cast_side else None
    mod_ref = rest.pop(0) if next_mod else None
    w_a, w_b = rest
    tn = o_ref.shape[1]

    def compute(w16):
        h = h_ref[...]
        chunks = range(0, tn, MXU_COLS)
        if next_mod:
            c_act = _silu_ref(cb_ref)
            slabs = list(range(0, wm_ref.shape[1], LANES))
        for idx, c0 in enumerate(chunks):
            cs = slice(c0, c0 + MXU_COLS)
            o_ref[:, cs] = _activate(_dot(h, w16[0][:, cs]), act).astype(o_ref.dtype)
            if next_mod:
                for j in slabs[idx::len(chunks)]:
                    mod_ref[:, j:j + LANES] = _ada_slab(c_act, wm_ref, bm_ref, j)
        if cast_side:
            side16_ref[...] = side_ref[...].astype(BF16)

    _run_streamed(((w_ref, w_a, w_b),), compute)


def _proj_act(h, w, layer, col0, width, act, name, side=None, ada=None):
    t, d = h.shape
    tm = min(1024, t)
    tn = 1024 if width % 1024 == 0 and col0 % 1024 == 0 else 512
    nb = width // tn
    n_inner = t // tm
    n_steps = nb * n_inner
    c0 = col0 // tn

    def step(jj, i):
        return jnp.where(jj == 0, 0, (jj - 1) * n_inner + i)

    in_specs = [_row_spec(tm, d), _stream_weight_spec(layer, d, n_inner, tn, nb, lambda j: c0 + j)]
    out_specs = [_tile_spec(tm, tn)]
    out_shape = [jax.ShapeDtypeStruct((t, width), BF16)]
    args = [h, w]
    if side is not None:
        _, side_rows, side_cols = side.shape
        blk = side_rows // n_steps
        assert blk * n_steps == side_rows and blk % 16 == 0
        in_specs.append(pl.BlockSpec((None, blk, side_cols), lambda jj, i: (layer, step(jj, i), 0)))
        args.append(side)
    if ada is not None:
        c, w_ada, b_ada = ada
        _, dm, n = w_ada.shape
        cols = n // n_steps
        assert cols * n_steps == n and cols % LANES == 0
        in_specs += [pl.BlockSpec((dm, LANES), lambda jj, i: (0, 0)),
                     pl.BlockSpec((None, dm, cols), lambda jj, i: (layer + 1, 0, step(jj, i))),
                     pl.BlockSpec((None, 1, cols), lambda jj, i: (layer + 1, 0, step(jj, i)))]
        args += [_lane_broadcast(c), w_ada, b_ada.reshape(-1, 1, n)]
    if side is not None:
        out_specs.append(pl.BlockSpec((blk, side_cols), lambda jj, i: (step(jj, i), 0)))
        out_shape.append(jax.ShapeDtypeStruct((side_rows, side_cols), BF16))
    if ada is not None:
        out_specs.append(pl.BlockSpec((1, cols), lambda jj, i: (0, step(jj, i))))
        out_shape.append(jax.ShapeDtypeStruct((1, n), F32))
    outs = pl.pallas_call(
        functools.partial(_proj_act_kernel, act=act, cast_side=side is not None, next_mod=ada is not None),
        grid=(nb + 1, n_inner),
        in_specs=in_specs,
        out_specs=out_specs,
        out_shape=out_shape,
        scratch_shapes=_w16_scratch(d, tn),
        compiler_params=_params(2),
        name=name,
    )(*args)
    return outs if len(outs) > 1 else outs[0]


CF_ROWS = 64
CF_COLS = 256


def _conformer_tile(is_first, uh_ref, u_ref, w_ref, b_ref, g_ref, be_ref, o_ref, sh_ref, y_ref):
    tm, width = u_ref.shape
    n_ext = tm + CF_HALO
    sh_ref[0, 0:CF_HALO, :] = jnp.where(is_first, 0.0, uh_ref[...])
    sh_ref[0, CF_HALO:, :] = u_ref[...]
    for s in range(1, SUBLANES):
        sh_ref[s, 0:n_ext - SUBLANES, :] = sh_ref[0, pl.ds(s, n_ext - SUBLANES), :]
    first = CF_HALO - (CF_KERNEL - 1)

    def row_chunk(rc, carry):
        r0 = pl.multiple_of(rc * CF_ROWS, CF_ROWS)
        for c0 in range(0, width, CF_COLS):
            cs = slice(c0, c0 + CF_COLS)
            acc = jnp.broadcast_to(b_ref[:, cs], (CF_ROWS, CF_COLS))
            for k in range(CF_KERNEL):
                shift = (first + k) % SUBLANES
                base = pl.multiple_of(r0 + (first + k - shift), SUBLANES)
                acc = acc + w_ref[k:k + 1, cs] * sh_ref[shift, pl.ds(base, CF_ROWS), cs]
            y_ref[pl.ds(r0, CF_ROWS), cs] = acc
        return carry

    lax.fori_loop(0, tm // CF_ROWS, row_chunk, 0)
    y = y_ref[...]
    mu = jnp.mean(y, axis=-1, keepdims=True)
    yc = y - mu
    var = jnp.mean(yc * yc, axis=-1, keepdims=True)
    z = yc * lax.rsqrt(var + EPS) * g_ref[...] + be_ref[...]
    o_ref[...] = (z * _sigmoid(z)).astype(o_ref.dtype)


def _conformer_kernel(uh_ref, u_ref, w_ref, b_ref, g_ref, be_ref, o_ref, sh_ref, y_ref):
    _conformer_tile(pl.program_id(0) == 0, uh_ref, u_ref, w_ref, b_ref, g_ref, be_ref, o_ref, sh_ref, y_ref)


def _conformer(u, conv_w, bias, ln_g, ln_b):
    t, width = u.shape
    tm = min(256, t)
    per = tm // CF_HALO
    row = pl.BlockSpec((1, width), lambda i: (0, 0))
    return pl.pallas_call(
        _conformer_kernel,
        grid=(t // tm,),
        in_specs=[pl.BlockSpec((CF_HALO, width), lambda i: (jnp.maximum(i * per - 1, 0), 0)),
                  pl.BlockSpec((tm, width), lambda i: (i, 0)),
                  pl.BlockSpec((CF_KERNEL, width), lambda i: (0, 0)), row, row, row],
        out_specs=pl.BlockSpec((tm, width), lambda i: (i, 0)),
        out_shape=jax.ShapeDtypeStruct((t, width), BF16),
        scratch_shapes=[pltpu.VMEM((SUBLANES, tm + CF_HALO, width), F32), pltpu.VMEM((tm, width), F32)],
        compiler_params=_params(1),
        name="conformer_conv",
    )(u, u, conv_w, bias, ln_g, ln_b)


def _attn_kernel(q_ref, kp_ref, kc_ref, vp_ref, vc_ref, o_ref, l_ref):
    n = ATTN_BLOCK
    n_blocks = q_ref.shape[0] // n
    k = jnp.concatenate([kp_ref[...], kc_ref[...]], axis=0)
    v = jnp.concatenate([vp_ref[...], vc_ref[...]], axis=0)
    qi = lax.broadcasted_iota(jnp.int32, (n, 2 * n), 0)
    kj = lax.broadcasted_iota(jnp.int32, (n, 2 * n), 1)
    band = (kj >= qi) & (kj <= qi + n)
    has_prev = pl.program_id(1) > 0
    l_ref[...] = jnp.zeros(l_ref.shape, F32)
    for jb in range(n_blocks):
        rows = slice(jb * n, (jb + 1) * n)
        keys = slice(jb * n, (jb + 2) * n)
        valid = band & ((kj >= n) | has_prev) if jb == 0 else band
        for h in range(DW_HEADS):
            sl = slice(h * HEAD_DIM, (h + 1) * HEAD_DIM)
            s = lax.dot_general(q_ref[rows, sl], k[keys, sl], (((1,), (1,)), ((), ())),
                                preferred_element_type=F32) * (HEAD_DIM ** -0.5)
            s = jnp.where(valid, s, -jnp.inf)
            m = jnp.max(s, axis=-1, keepdims=True)
            p = jnp.exp(s - m)
            den = jnp.sum(p, axis=-1, keepdims=True)
            o_ref[rows, sl] = _dot((p / den).astype(BF16), v[keys, sl])
            l_ref[rows, h:h + 1] = m + jnp.log(den)


def _attention_group(qkv):
    _, dilation, sub, gw = qkv.shape
    n = ATTN_BLOCK
    assert sub % n == 0
    per_step = min(4, sub // n)
    tq = per_step * n
    assert sub % tq == 0

    def cur(sec):
        return pl.BlockSpec((None, None, tq, gw), lambda r, b: (sec, r, b, 0))

    def prev(sec):
        return pl.BlockSpec((None, None, n, gw), lambda r, b: (sec, r, jnp.maximum(b * per_step - 1, 0), 0))

    return pl.pallas_call(
        _attn_kernel,
        grid=(dilation, sub // tq),
        in_specs=[cur(0), prev(1), cur(1), prev(2), cur(2)],
        out_specs=[pl.BlockSpec((None, tq, gw), lambda r, b: (r, b, 0)),
                   pl.BlockSpec((None, tq, LANES), lambda r, b: (r, b, 0))],
        out_shape=[jax.ShapeDtypeStruct((dilation, sub, gw), F32),
                   jax.ShapeDtypeStruct((dilation, sub, LANES), F32)],
        compiler_params=_params(2),
        name=f"dilated_attn_d{dilation}",
    )(qkv, qkv, qkv, qkv, qkv)


def _combine_kernel(o0_ref, o1_ref, o2_ref, l0_ref, l1_ref, l2_ref, out_ref, lnat_ref, onat_ref):
    o_refs = (o0_ref, o1_ref, o2_ref)
    l_refs = (l0_ref, l1_ref, l2_ref)
    tm = out_ref.shape[0]

    def natural(dst_ref, src, dilation):
        rows = tm // dilation
        for r in range(dilation):
            dst_ref[pl.ds(r, rows, stride=dilation), :] = src(r)

    lse = []
    for g, l_ref in enumerate(l_refs):
        dilation = l_ref.shape[0]
        if dilation == 1:
            lse.append(l_ref[0])
        else:
            natural(lnat_ref.at[g], lambda r, l_ref=l_ref: l_ref[r], dilation)
            lse.append(lnat_ref[g])
    m = jnp.maximum(jnp.maximum(lse[0], lse[1]), lse[2])
    e = [jnp.exp(l - m) for l in lse]
    den = e[0] + e[1] + e[2]
    w = [x / den for x in e]
    for h in range(DW_HEADS):
        sl = slice(h * HEAD_DIM, (h + 1) * HEAD_DIM)
        acc = None
        for g, o_ref in enumerate(o_refs):
            dilation = o_ref.shape[0]
            if dilation == 1:
                o_nat = o_ref[0, :, sl]
            else:
                natural(onat_ref, lambda r, o_ref=o_ref: o_ref[r, :, sl], dilation)
                o_nat = onat_ref[...]
            term = w[g][:, h:h + 1] * o_nat
            acc = term if acc is None else acc + term
        out_ref[:, sl] = acc.astype(out_ref.dtype)


def _combine(outs, lses):
    gw = outs[0].shape[-1]
    t = outs[0].shape[0] * outs[0].shape[1]
    tm = min(512, t)

    def spec(a):
        dil = a.shape[0]
        return pl.BlockSpec((dil, tm // dil, a.shape[-1]), lambda i: (0, i, 0))

    return pl.pallas_call(
        _combine_kernel,
        grid=(t // tm,),
        in_specs=[spec(a) for a in outs] + [spec(a) for a in lses],
        out_specs=pl.BlockSpec((tm, gw), lambda i: (i, 0)),
        out_shape=jax.ShapeDtypeStruct((t, gw), BF16),
        scratch_shapes=[pltpu.VMEM((N_DW_GROUPS, tm, LANES), F32), pltpu.VMEM((tm, LANES), F32)],
        compiler_params=_params(1),
        name="attn_combine",
    )(*outs, *lses)


def _merge_kernel(za_ref, zb_ref, zc_ref, wa_ref, wb_ref, wc_ref, ga_ref, gb_ref, gc_ref, o_ref,
                  wa_a, wa_b, wb_a, wb_b, wc_a, wc_b):
    def compute(w16):
        wa16, wb16, wc16 = w16
        m = ga_ref[...].astype(F32) * _dot(za_ref[...], wa16[...])
        m = m + gb_ref[...].astype(F32) * _dot(zb_ref[...], wb16[...])
        m = m + gc_ref[...].astype(F32) * _dot(zc_ref[...], wc16[...])
        o_ref[...] = m.astype(o_ref.dtype)

    _run_streamed(((wa_ref, wa_a, wa_b), (wb_ref, wb_a, wb_b), (wc_ref, wc_a, wc_b)), compute)


def _merge(za, zb, zc, wa, wb, wc, layer, gates):
    t = za.shape[0]
    d = wa.shape[-1]
    tm, tn = min(1024, t), min(512, d)
    nb = d // tn
    n_inner = t // tm

    def wgt(a):
        return _stream_weight_spec(layer, a.shape[1], n_inner, tn, nb, lambda j: j)

    def gate(sec):
        return _tile_spec(tm, tn, lambda j: sec * nb + j)

    return pl.pallas_call(
        _merge_kernel,
        grid=(nb + 1, n_inner),
        in_specs=[_row_spec(tm, za.shape[1]), _row_spec(tm, zb.shape[1]), _row_spec(tm, zc.shape[1]),
                  wgt(wa), wgt(wb), wgt(wc), gate(0), gate(1), gate(2)],
        out_specs=_tile_spec(tm, tn),
        out_shape=jax.ShapeDtypeStruct((t, d), BF16),
        scratch_shapes=[buf for a in (wa, wb, wc) for buf in _w16_scratch(a.shape[1], tn)],
        compiler_params=_params(2),
        name="merge_branches",
    )(za, zb, zc, wa, wb, wc, gates, gates, gates)


def _out_proj_kernel(a_ref, w_ref, x_ref, g_ref, o_ref, w_a, w_b):
    def compute(w16):
        o_ref[...] = x_ref[...] + g_ref[...] * _dot(a_ref[...], w16[0][...])

    _run_streamed(((w_ref, w_a, w_b),), compute)


def _out_proj(a, w, layer, x, gate):
    t, kdim = a.shape
    d = w.shape[-1]
    tm, tn = min(1024, t), min(1024, d)
    nb = d // tn
    n_inner = t // tm
    return pl.pallas_call(
        _out_proj_kernel,
        grid=(nb + 1, n_inner),
        in_specs=[_row_spec(tm, kdim), _stream_weight_spec(layer, kdim, n_inner, tn, nb, lambda j: j),
                  _tile_spec(tm, tn), _col_spec(1, tn)],
        out_specs=_tile_spec(tm, tn),
        out_shape=jax.ShapeDtypeStruct((t, d), F32),
        scratch_shapes=_w16_scratch(kdim, tn),
        compiler_params=_params(2),
        name="out_proj_resid",
    )(a, w, x, gate)


def _resid_kernel(a_ref, w_ref, x_ref, g_ref, o_ref, acc_ref):
    kk = pl.program_id(2)
    last = pl.num_programs(2) - 1

    @pl.when(kk == 0)
    def _():
        acc_ref[...] = _dot(a_ref[...], w_ref[...])

    @pl.when((kk > 0) & (kk < last))
    def _():
        acc_ref[...] += _dot(a_ref[...], w_ref[...])

    @pl.when(kk == last)
    def _():
        o_ref[...] = x_ref[...] + g_ref[...] * (acc_ref[...] + _dot(a_ref[...], w_ref[...]))


def _resid_matmul(a, w, x, gate, name):
    t, kdim = a.shape
    d = w.shape[1]
    tm, tn, tk = min(1024, t), min(1024, d), min(4096, kdim // 2)
    return pl.pallas_call(
        _resid_kernel,
        grid=(d // tn, t // tm, kdim // tk),
        in_specs=[pl.BlockSpec((tm, tk), lambda j, i, k: (i, k)),
                  pl.BlockSpec((tk, tn), lambda j, i, k: (k, j)),
                  pl.BlockSpec((tm, tn), lambda j, i, k: (i, j)),
                  pl.BlockSpec((1, tn), lambda j, i, k: (0, j))],
        out_specs=pl.BlockSpec((tm, tn), lambda j, i, k: (i, j)),
        out_shape=jax.ShapeDtypeStruct((t, d), F32),
        scratch_shapes=[pltpu.VMEM((tm, tn), F32)],
        compiler_params=_params(3),
        name=name,
    )(a, w, x, gate)


def kernel(x, c, positions, w_ada, b_ada, g_mix, w_in, conv_a, conv_b, conv_b_bias, ln_cf_g, ln_cf_b,
           w_out_a, w_out_b, w_out_c, w_o, g_mlp, w_mlp1, w_mlp2, g_final):
    b, t, d = x.shape
    assert b == 1, "kernel written for a single sequence"
    depth = w_ada.shape[0]
    d_ff = w_mlp1.shape[-1]
    xs = x[0]

    mod = _ada(c, w_ada, b_ada, 0)
    tables = _rope_tables(positions[0])

    col_sc = 0
    col_cf = col_sc + 3 * SC_WIDTH
    col_q = col_cf + 2 * CF_WIDTH
    col_gate = col_q + 3 * DW_WIDTH

    for l in range(depth):
        shift1, scale1, gate1, shift2, scale2, gate2 = (mod[:, i * d:(i + 1) * d] for i in range(N_MOD))

        h = _norm_mod(xs, g_mix[l][None, :], scale1, shift1)
        z_a = _proj_sconv(h, w_in, l, conv_a[l], col_sc)
        u = _proj_glu(h, w_in, l, col_cf)

        outs, lses = [], []
        for g, (window, dilation) in enumerate(DW_PATTERNS):
            assert window // dilation == ATTN_BLOCK
            o_g, lse_g = _attention_group(_proj_qkv(h, w_in, l, tables, col_q, g, dilation))
            outs.append(o_g)
            lses.append(lse_g)
        z_c = _combine(outs, lses)

        z_b = _conformer(u, conv_b[l], conv_b_bias[l][None, :], ln_cf_g[l][None, :], ln_cf_b[l][None, :])
        if l + 1 < depth:
            gates, next_mod = _proj_act(h, w_in, l, col_gate, 3 * d, "sigmoid", "proj_gates",
                                        ada=(c, w_ada, b_ada))
        else:
            gates = _proj_act(h, w_in, l, col_gate, 3 * d, "sigmoid", "proj_gates")
        merged = _merge(z_a, z_b, z_c, w_out_a, w_out_b, w_out_c, l, gates)
        xs = _out_proj(merged, w_o, l, xs, gate1)

        h2 = _norm_mod(xs, g_mlp[l][None, :], scale2, shift2)
        a, w_down = _proj_act(h2, w_mlp1, l, 0, d_ff, "relu2", "mlp_up", side=w_mlp2)
        xs = _resid_matmul(a, w_down, xs, gate2, "mlp_down_resid")
        if l + 1 < depth:
            mod = next_mod

    return _final_norm(xs, g_final[None, :])[None]
```
